```python
import math
import jax
import jax.numpy as jnp
from jax import lax
import numpy as np

D_MODEL = 4096
BATCH = 1
SEQ = 16384
DEPTH = 4

CHUNK = 64
Q_BLOCK = 128
N_MIXERS = 3
N_FOX_LAYERS = (DEPTH + 2) // 3
N_LRU_LAYERS = (DEPTH + 1) // 3
N_MLSTM_LAYERS = DEPTH // 3
RMS_EPS = 1e-6
NEG_INF = -1e30

FOX_HEADS = 32
FOX_HEAD_DIM = D_MODEL // FOX_HEADS
FOX_WIDTH = FOX_HEADS * FOX_HEAD_DIM
FOX_IN = 4 * FOX_WIDTH + FOX_HEADS

LRU_WIDTH = D_MODEL
LRU_BLOCKS = 16
LRU_BLOCK_SIZE = LRU_WIDTH // LRU_BLOCKS
CONV_WIDTH = 4
LRU_C = 8.0
LRU_IN = 2 * LRU_WIDTH

MLSTM_HEADS = 8
MLSTM_V_DIM = D_MODEL // MLSTM_HEADS
MLSTM_QK_DIM = MLSTM_V_DIM // 2
MLSTM_QK_W = MLSTM_HEADS * MLSTM_QK_DIM
MLSTM_V_W = MLSTM_HEADS * MLSTM_V_DIM
MLSTM_IN = 2 * MLSTM_QK_W + 3 * MLSTM_V_W + 2 * MLSTM_HEADS

kernel_name = "fox_rglru_mlstm_interleaved_trunk"


def _rms_norm(x, g):
    xf = x.astype(jnp.float32)
    y = xf * lax.rsqrt(jnp.mean(xf * xf, axis=-1, keepdims=True) + RMS_EPS)
    return (y * g.astype(jnp.float32)).astype(x.dtype)


def _heads(t, n_heads):
    b, s, _ = t.shape
    return t.reshape(b, s, n_heads, -1).transpose(0, 2, 1, 3)


def _merge(t):
    b, h, s, d = t.shape
    return t.transpose(0, 2, 1, 3).reshape(b, s, h * d)


def _fox_attention(q, k, v, logf):
    b, h, s, dh = q.shape
    nb = s // Q_BLOCK
    scale = dh ** -0.5
    F = jnp.cumsum(logf, axis=-1)
    qb = q.reshape(b, h, nb, Q_BLOCK, dh).transpose(2, 0, 1, 3, 4)
    Fb = F.reshape(b, h, nb, Q_BLOCK).transpose(2, 0, 1, 3)
    kpos = jnp.arange(s)

    def one_block(args):
        q_i, F_i, blk = args
        qpos = blk * Q_BLOCK + jnp.arange(Q_BLOCK)
        logits = (jnp.einsum('bhqd,bhkd->bhqk', q_i, k).astype(jnp.float32) * scale
                  + F_i[..., :, None] - F[..., None, :])
        logits = jnp.where(kpos[None, :] <= qpos[:, None], logits, NEG_INF)
        p = jax.nn.softmax(logits, axis=-1)
        return jnp.einsum('bhqk,bhkd->bhqd', p.astype(v.dtype), v)

    out = lax.map(one_block, (qb, Fb, jnp.arange(nb)))
    return out.transpose(1, 2, 0, 3, 4).reshape(b, h, s, dh)


def _fox_branch(u, w_in, b_f, w_out):
    p = u @ w_in
    q, k, v, z, f_pre = jnp.split(
        p, [FOX_WIDTH, 2 * FOX_WIDTH, 3 * FOX_WIDTH, 4 * FOX_WIDTH], axis=-1)
    logf = jax.nn.log_sigmoid((f_pre + b_f).astype(jnp.float32)).transpose(0, 2, 1)
    o = _fox_attention(_heads(q, FOX_HEADS), _heads(k, FOX_HEADS),
                       _heads(v, FOX_HEADS), logf)
    return (_merge(o) * jax.nn.silu(z)) @ w_out


def _causal_conv(x, w, b):
    kw = w.shape[0]
    s = x.shape[1]
    xp = jnp.pad(x, ((0, 0), (kw - 1, 0), (0, 0)))
    return sum(xp[:, j:j + s] * w[j] for j in range(kw)) + b


def _lin_combine(left, right):
    a_l, b_l = left
    a_r, b_r = right
    return a_l * a_r, a_r * b_l + b_r


def _lru_branch(u, w_in, conv_w, conv_b, w_a, b_a, w_x, b_x, lam, w_out):
    bsz, s, _ = u.shape
    p = u @ w_in
    xb, z = jnp.split(p, [LRU_WIDTH], axis=-1)
    xc = _causal_conv(xb, conv_w, conv_b)
    xr = xc.reshape(bsz, s, LRU_BLOCKS, LRU_BLOCK_SIZE)
    r = jax.nn.sigmoid((jnp.einsum('bsnc,ncd->bsnd', xr, w_a) + b_a)
                       .astype(jnp.float32)).reshape(bsz, s, LRU_WIDTH)
    ig = jax.nn.sigmoid((jnp.einsum('bsnc,ncd->bsnd', xr, w_x) + b_x)
                        .astype(jnp.float32)).reshape(bsz, s, LRU_WIDTH)
    log_a = -LRU_C * r * jax.nn.softplus(-lam.astype(jnp.float32))
    a = jnp.exp(log_a)
    inp = jnp.sqrt(-jnp.expm1(2.0 * log_a)) * (ig * xc.astype(jnp.float32))
    _, h = lax.associative_scan(_lin_combine, (a, inp), axis=1)
    return (h.astype(u.dtype) * jax.nn.silu(z)) @ w_out


def _mlstm_chunkwise(q, k, v, i_pre, logf):
    b, h, s, dk = q.shape
    dv = v.shape[-1]
    L = CHUNK
    nc = s // L
    q = q * (dk ** -0.5)

    def to_chunks(t):
        return jnp.moveaxis(t.reshape(b, h, nc, L, *t.shape[3:]), 2, 0)

    xs = (to_chunks(q), to_chunks(k), to_chunks(v), to_chunks(i_pre), to_chunks(logf))
    causal = jnp.tril(jnp.ones((L, L), dtype=bool))

    def step(carry, xs_c):
        C, n, m = carry
        qc, kc, vc, ic, fc = xs_c
        bcum = jnp.cumsum(fc, axis=-1)
        g = bcum[..., -1]
        Dm = jnp.where(causal, bcum[..., :, None] - bcum[..., None, :] + ic[..., None, :], NEG_INF)
        inter = bcum + m[..., None]
        m_row = jnp.maximum(inter, jnp.max(Dm, axis=-1))
        Wts = jnp.exp(Dm - m_row[..., None])
        sc = jnp.einsum('bhtd,bhsd->bhts', qc, kc) * Wts
        inter_w = jnp.exp(inter - m_row)
        num = (jnp.einsum('bhts,bhsv->bhtv', sc, vc)
               + inter_w[..., None] * jnp.einsum('bhvd,bhtd->bhtv', C, qc))
        den = jnp.sum(sc, axis=-1) + inter_w * jnp.einsum('bhd,bhtd->bht', n, qc)
        h_c = num / jnp.maximum(jnp.abs(den), jnp.exp(-m_row))[..., None]
        a_s = g[..., None] - bcum + ic
        m_new = jnp.maximum(g + m, jnp.max(a_s, axis=-1))
        wk = jnp.exp(a_s - m_new[..., None])
        decay = jnp.exp(g + m - m_new)
        C_new = decay[..., None, None] * C + jnp.einsum('bhsv,bhsd->bhvd', vc * wk[..., None], kc)
        n_new = decay[..., None] * n + jnp.einsum('bhs,bhsd->bhd', wk, kc)
        return (C_new, n_new, m_new), h_c

    init = (jnp.zeros((b, h, dv, dk), jnp.float32),
            jnp.zeros((b, h, dk), jnp.float32),
            jnp.zeros((b, h), jnp.float32))
    _, hs = lax.scan(step, init, xs)
    return jnp.moveaxis(hs, 0, 2).reshape(b, h, s, dv)


def _mlstm_branch(u, w_in, b_i, b_f, w_out):
    p = u @ w_in
    o1 = MLSTM_QK_W
    o2 = o1 + MLSTM_QK_W
    o3 = o2 + MLSTM_V_W
    o4 = o3 + MLSTM_V_W
    o5 = o4 + MLSTM_HEADS
    o6 = o5 + MLSTM_HEADS
    q, k, v, o_pre, i_pre, f_pre, z = jnp.split(p, [o1, o2, o3, o4, o5, o6], axis=-1)
    f32 = jnp.float32
    i_g = (i_pre + b_i).astype(f32).transpose(0, 2, 1)
    logf = jax.nn.log_sigmoid((f_pre + b_f).astype(f32)).transpose(0, 2, 1)
    hc = _mlstm_chunkwise(_heads(q, MLSTM_HEADS).astype(f32), _heads(k, MLSTM_HEADS).astype(f32),
                          _heads(v, MLSTM_HEADS).astype(f32), i_g, logf)
    y = jax.nn.sigmoid(o_pre) * _merge(hc).astype(u.dtype) * jax.nn.silu(z)
    return y @ w_out


def setup_inputs(seed: int = 0) -> dict:
    key = jax.random.key(seed)
    ks = jax.random.split(key, 24)
    f32 = jnp.float32
    nrm = lambda k, shape, sc: jax.random.normal(k, shape, f32) * sc
    x = jax.random.normal(ks[0], (BATCH, SEQ, D_MODEL), f32)
    norm_g = 1.0 + nrm(ks[1], (DEPTH, D_MODEL), 0.02)
    fox_w_in = nrm(ks[2], (N_FOX_LAYERS, D_MODEL, FOX_IN), D_MODEL ** -0.5)
    fox_b_f = jnp.linspace(2.0, 8.0, FOX_HEADS, dtype=f32)[None] + nrm(ks[3], (N_FOX_LAYERS, FOX_HEADS), 0.1)
    fox_w_out = nrm(ks[4], (N_FOX_LAYERS, FOX_WIDTH, D_MODEL), FOX_WIDTH ** -0.5)
    lru_w_in = nrm(ks[5], (N_LRU_LAYERS, D_MODEL, LRU_IN), D_MODEL ** -0.5)
    lru_conv_w = nrm(ks[6], (N_LRU_LAYERS, CONV_WIDTH, LRU_WIDTH), CONV_WIDTH ** -0.5)
    lru_conv_b = nrm(ks[7], (N_LRU_LAYERS, LRU_WIDTH), 0.01)
    lru_w_a = nrm(ks[8], (N_LRU_LAYERS, LRU_BLOCKS, LRU_BLOCK_SIZE, LRU_BLOCK_SIZE), LRU_BLOCK_SIZE ** -0.5)
    lru_b_a = nrm(ks[9], (N_LRU_LAYERS, LRU_BLOCKS, LRU_BLOCK_SIZE), 0.01)
    lru_w_x = nrm(ks[10], (N_LRU_LAYERS, LRU_BLOCKS, LRU_BLOCK_SIZE, LRU_BLOCK_SIZE), LRU_BLOCK_SIZE ** -0.5)
    lru_b_x = nrm(ks[11], (N_LRU_LAYERS, LRU_BLOCKS, LRU_BLOCK_SIZE), 0.01)
    a_c = jax.random.uniform(ks[12], (N_LRU_LAYERS, LRU_WIDTH), f32, 0.9, 0.999)
    a0 = a_c ** (1.0 / LRU_C)
    lru_lambda = jnp.log(a0) - jnp.log1p(-a0)
    lru_w_out = nrm(ks[13], (N_LRU_LAYERS, LRU_WIDTH, D_MODEL), LRU_WIDTH ** -0.5)
    mlstm_w_in = nrm(ks[14], (N_MLSTM_LAYERS, D_MODEL, MLSTM_IN), D_MODEL ** -0.5)
    mlstm_b_i = nrm(ks[15], (N_MLSTM_LAYERS, MLSTM_HEADS), 0.1)
    mlstm_b_f = jnp.linspace(3.0, 6.0, MLSTM_HEADS, dtype=f32)[None] + nrm(ks[16], (N_MLSTM_LAYERS, MLSTM_HEADS), 0.1)
    mlstm_w_out = nrm(ks[17], (N_MLSTM_LAYERS, MLSTM_V_W, D_MODEL), MLSTM_V_W ** -0.5)
    final_norm_g = 1.0 + nrm(ks[18], (D_MODEL,), 0.02)
    return {"x": x, "norm_g": norm_g,
            "fox_w_in": fox_w_in, "fox_b_f": fox_b_f, "fox_w_out": fox_w_out,
            "lru_w_in": lru_w_in, "lru_conv_w": lru_conv_w, "lru_conv_b": lru_conv_b,
            "lru_w_a": lru_w_a, "lru_b_a": lru_b_a, "lru_w_x": lru_w_x, "lru_b_x": lru_b_x,
            "lru_lambda": lru_lambda, "lru_w_out": lru_w_out,
            "mlstm_w_in": mlstm_w_in, "mlstm_b_i": mlstm_b_i, "mlstm_b_f": mlstm_b_f,
            "mlstm_w_out": mlstm_w_out, "final_norm_g": final_norm_g}


def reference(x, norm_g, fox_w_in, fox_b_f, fox_w_out,
              lru_w_in, lru_conv_w, lru_conv_b, lru_w_a, lru_b_a, lru_w_x, lru_b_x,
              lru_lambda, lru_w_out,
              mlstm_w_in, mlstm_b_i, mlstm_b_f, mlstm_w_out, final_norm_g):
    h = x
    for i in range(DEPTH):
        j = i // N_MIXERS
        u = _rms_norm(h, norm_g[i])
        kind = i % N_MIXERS
        if kind == 0:
            y = _fox_branch(u, fox_w_in[j], fox_b_f[j], fox_w_out[j])
        elif kind == 1:
            y = _lru_branch(u, lru_w_in[j], lru_conv_w[j], lru_conv_b[j], lru_w_a[j],
                            lru_b_a[j], lru_w_x[j], lru_b_x[j], lru_lambda[j], lru_w_out[j])
        else:
            y = _mlstm_branch(u, mlstm_w_in[j], mlstm_b_i[j], mlstm_b_f[j], mlstm_w_out[j])
        h = h + y.astype(h.dtype)
    return _rms_norm(h, final_norm_g)
```

```python
import functools

import jax
import jax.numpy as jnp
from jax import lax
from jax.experimental import pallas as pl
from jax.experimental.pallas import tpu as pltpu

F32 = jnp.float32
BF16 = jnp.bfloat16

RMS_EPS = 1e-6
NEG_INF = -1e30
N_MIXERS = 3

FOX_HEAD_DIM = 128
LRU_BLOCK_SIZE = 256
CONV_WIDTH = 4
LRU_C = 8.0
MLSTM_V_DIM = 512
MLSTM_QK_DIM = 256

LANES = 128
SUBLANES = 8
V7X_VMEM_LIMIT_BYTES = 56 * 1024 * 1024

NORM_ROWS = 256
MM_TM = 1024
MM_TN = 1024
GATE_ROWS = 512
ATTN_TQ = 512
LRU_ROWS = 256
MLSTM_CHUNK = 256


def _params(*sem):
    return pltpu.CompilerParams(dimension_semantics=sem,
                                vmem_limit_bytes=V7X_VMEM_LIMIT_BYTES)


def _norm_kernel(x_ref, g_ref, o_ref):
    x = x_ref[...]
    y = x * lax.rsqrt(jnp.mean(x * x, axis=-1, keepdims=True) + RMS_EPS)
    o_ref[...] = (y * g_ref[...]).astype(o_ref.dtype)


def rms_norm(x, g, out_dtype):
    s, d = x.shape
    tr = min(NORM_ROWS, s)
    return pl.pallas_call(
        _norm_kernel,
        out_shape=jax.ShapeDtypeStruct((s, d), out_dtype),
        grid=(s // tr,),
        in_specs=[pl.BlockSpec((tr, d), lambda i: (i, 0)),
                  pl.BlockSpec((1, d), lambda i: (0, 0))],
        out_specs=pl.BlockSpec((tr, d), lambda i: (i, 0)),
        compiler_params=_params("parallel"),
        name="rms_norm",
    )(x, g.reshape(1, d))


def _mm_kernel(a_ref, w_ref, o_ref):
    o_ref[...] = jnp.dot(a_ref[...], w_ref[...],
                         preferred_element_type=F32).astype(o_ref.dtype)


def _mm_res_kernel(a_ref, w_ref, r_ref, o_ref):
    o_ref[...] = r_ref[...] + jnp.dot(a_ref[...], w_ref[...],
                                      preferred_element_type=F32)


def matmul(a, w, out_dtype, residual=None):
    m, k = a.shape
    n = w.shape[1]
    tm = min(MM_TM, m)
    tn = min(MM_TN, n)
    in_specs = [pl.BlockSpec((tm, k), lambda i, j: (i, 0)),
                pl.BlockSpec((k, tn), lambda i, j: (0, j))]
    args = [a, w]
    body = _mm_kernel
    if residual is not None:
        in_specs.append(pl.BlockSpec((tm, tn), lambda i, j: (i, j)))
        args.append(residual)
        body = _mm_res_kernel
    return pl.pallas_call(
        body,
        out_shape=jax.ShapeDtypeStruct((m, n), out_dtype),
        grid=(m // tm, n // tn),
        in_specs=in_specs,
        out_specs=pl.BlockSpec((tm, tn), lambda i, j: (i, j)),
        compiler_params=_params("parallel", "parallel"),
        name="matmul_res" if residual is not None else "matmul",
    )(*args)


def _lower_tri(n):
    r = lax.broadcasted_iota(jnp.int32, (n, n), 0)
    c = lax.broadcasted_iota(jnp.int32, (n, n), 1)
    return (c <= r).astype(F32)


def _upper_tri(n):
    r = lax.broadcasted_iota(jnp.int32, (n, n), 0)
    c = lax.broadcasted_iota(jnp.int32, (n, n), 1)
    return (r <= c).astype(F32)


def _dot_f32(a, b):
    return jnp.dot(a, b, precision=lax.Precision.HIGHEST, preferred_element_type=F32)


def _fox_gate_kernel(f_ref, b_ref, fcol_ref, frow_ref, carry_ref):
    @pl.when(pl.program_id(0) == 0)
    def _():
        carry_ref[...] = jnp.zeros_like(carry_ref)

    tb = f_ref.shape[0]
    logf = jax.nn.log_sigmoid(f_ref[...] + b_ref[...])
    cs = _dot_f32(_lower_tri(tb), logf) + carry_ref[...]
    fcol_ref[...] = cs
    frow_ref[...] = cs.T
    carry_ref[...] = cs[tb - 1:tb, :]


def fox_gates(f_pre, b_pad):
    s = f_pre.shape[0]
    tb = min(GATE_ROWS, s)
    return pl.pallas_call(
        _fox_gate_kernel,
        out_shape=(jax.ShapeDtypeStruct((s, LANES), F32),
                   jax.ShapeDtypeStruct((LANES, s), F32)),
        grid=(s // tb,),
        in_specs=[pl.BlockSpec((tb, LANES), lambda i: (i, 0)),
                  pl.BlockSpec((1, LANES), lambda i: (0, 0))],
        out_specs=(pl.BlockSpec((tb, LANES), lambda i: (i, 0)),
                   pl.BlockSpec((LANES, tb), lambda i: (0, i))),
        scratch_shapes=[pltpu.VMEM((1, LANES), F32)],
        compiler_params=_params("arbitrary"),
        name="fox_gates",
    )(f_pre, b_pad)


def _fox_attn_kernel(q_ref, k_ref, v_ref, z_ref, fcol_ref, frow_ref, o_ref,
                     m_sc, l_sc, acc_sc, *, scale):
    h = pl.program_id(0)
    i = pl.program_id(1)
    tq = q_ref.shape[0]
    q = q_ref[...]
    lane = lax.broadcasted_iota(jnp.int32, (tq, LANES), 1)
    fq = jnp.sum(jnp.where(lane == h, fcol_ref[...], 0.0), axis=1, keepdims=True)

    m_sc[...] = jnp.full_like(m_sc, NEG_INF)
    l_sc[...] = jnp.zeros_like(l_sc)
    acc_sc[...] = jnp.zeros_like(acc_sc)

    def block(j, masked):
        off = pl.multiple_of(j * tq, tq)
        k = k_ref[pl.ds(off, tq), :]
        v = v_ref[pl.ds(off, tq), :]
        s = lax.dot_general(q, k, (((1,), (1,)), ((), ())), preferred_element_type=F32)
        s = s * scale + fq - frow_ref[:, pl.ds(off, tq)]
        if masked:
            row = lax.broadcasted_iota(jnp.int32, (tq, tq), 0)
            col = lax.broadcasted_iota(jnp.int32, (tq, tq), 1)
            s = jnp.where(col <= row, s, NEG_INF)
        m_prev = m_sc[...]
        m_new = jnp.maximum(m_prev, jnp.max(s, axis=1, keepdims=True))
        alpha = jnp.exp(m_prev - m_new)
        p = jnp.exp(s - m_new)
        l_sc[...] = alpha * l_sc[...] + jnp.sum(p, axis=1, keepdims=True)
        acc_sc[...] = alpha * acc_sc[...] + jnp.dot(p.astype(BF16), v,
                                                    preferred_element_type=F32)
        m_sc[...] = m_new

    def off_diag(j, carry):
        block(j, False)
        return carry

    lax.fori_loop(0, i, off_diag, 0)
    block(i, True)

    o = acc_sc[...] / l_sc[...]
    o_ref[...] = (o * jax.nn.silu(z_ref[...])).astype(o_ref.dtype)


def fox_attention(qkv, z, fcol, frow3, n_heads):
    s = qkv.shape[0]
    dh = FOX_HEAD_DIM
    tq = min(ATTN_TQ, s)
    kernel = functools.partial(_fox_attn_kernel, scale=dh ** -0.5)
    return pl.pallas_call(
        kernel,
        out_shape=jax.ShapeDtypeStruct((s, n_heads * dh), BF16),
        grid=(n_heads, s // tq),
        in_specs=[pl.BlockSpec((tq, dh), lambda h, i: (i, h)),
                  pl.BlockSpec((s, dh), lambda h, i: (0, n_heads + h)),
                  pl.BlockSpec((s, dh), lambda h, i: (0, 2 * n_heads + h)),
                  pl.BlockSpec((tq, dh), lambda h, i: (i, h)),
                  pl.BlockSpec((tq, LANES), lambda h, i: (i, 0)),
                  pl.BlockSpec((None, 1, s), lambda h, i: (h, 0, 0))],
        out_specs=pl.BlockSpec((tq, dh), lambda h, i: (i, h)),
        scratch_shapes=[pltpu.VMEM((tq, 1), F32), pltpu.VMEM((tq, 1), F32),
                        pltpu.VMEM((tq, dh), F32)],
        compiler_params=_params("parallel", "parallel"),
        name="fox_attention",
    )(qkv, qkv, qkv, z, fcol, frow3)


def fox_branch(u, w_in, b_f, w_out, h_res):
    s, d = u.shape
    n_heads = b_f.shape[0]
    width = n_heads * FOX_HEAD_DIM
    assert n_heads <= LANES
    w_qkv = w_in[:, :3 * width].astype(BF16)
    w_z = w_in[:, 3 * width:4 * width].astype(BF16)
    w_f = jnp.pad(w_in[:, 4 * width:], ((0, 0), (0, LANES - n_heads))).astype(BF16)
    b_pad = jnp.pad(b_f, (0, LANES - n_heads)).reshape(1, LANES)
    qkv = matmul(u, w_qkv, BF16)
    z = matmul(u, w_z, F32)
    f_pre = matmul(u, w_f, F32)
    fcol, frow = fox_gates(f_pre, b_pad)
    frow3 = frow[:n_heads].reshape(n_heads, 1, s)
    gated = fox_attention(qkv, z, fcol, frow3, n_heads)
    return matmul(gated, w_out.astype(BF16), F32, residual=h_res)


def _lru_kernel(x_ref, halo_ref, z_ref, cw_ref, cb_ref, wa_ref, ba_ref, wx_ref, bx_ref,
                lam_ref, o_ref, a_sc, b_sc, h_sc, state_sc):
    i = pl.program_id(0)
    t, width = x_ref.shape
    n_blocks = wa_ref.shape[0]
    bs = LRU_BLOCK_SIZE

    @pl.when(i == 0)
    def _():
        state_sc[...] = jnp.zeros_like(state_sc)

    halo = jnp.where(i == 0, 0.0, halo_ref[...])
    ext = jnp.concatenate([halo, x_ref[...]], axis=0)
    xc = cb_ref[...] + x_ref[...] * cw_ref[CONV_WIDTH - 1:CONV_WIDTH, :]
    for j in range(CONV_WIDTH - 1):
        shifted = pltpu.roll(ext, CONV_WIDTH - 1 - j, axis=0)[SUBLANES:, :]
        xc = xc + shifted * cw_ref[j:j + 1, :]

    for n in range(n_blocks):
        sl = slice(n * bs, (n + 1) * bs)
        xn = xc[:, sl]
        xn16 = xn.astype(BF16)
        r = jax.nn.sigmoid(jnp.dot(xn16, wa_ref[n], preferred_element_type=F32) + ba_ref[:, sl])
        ig = jax.nn.sigmoid(jnp.dot(xn16, wx_ref[n], preferred_element_type=F32) + bx_ref[:, sl])
        log_a = -LRU_C * r * jax.nn.softplus(-lam_ref[:, sl])
        a = jnp.exp(log_a)
        a_sc[:, sl] = a
        b_sc[:, sl] = jnp.sqrt(-jnp.tanh(log_a) * (a * a + 1.0)) * (ig * xn)

    def step(r, hprev):
        hnew = a_sc[pl.ds(r, 1), :] * hprev + b_sc[pl.ds(r, 1), :]
        h_sc[pl.ds(r, 1), :] = hnew
        return hnew

    state_sc[...] = lax.fori_loop(0, t, step, state_sc[...], unroll=8)
    o_ref[...] = (h_sc[...] * jax.nn.silu(z_ref[...])).astype(o_ref.dtype)


def lru_mixer(p, conv_w, conv_b, w_a, b_a, w_x, b_x, lam):
    s = p.shape[0]
    width = p.shape[1] // 2
    n_blocks = w_a.shape[0]
    t = min(LRU_ROWS, s)
    halo_blocks = t // SUBLANES
    row = lambda v: v.reshape(1, width)
    const2 = lambda i: (0, 0)
    const3 = lambda i: (0, 0, 0)
    return pl.pallas_call(
        _lru_kernel,
        out_shape=jax.ShapeDtypeStruct((s, width), BF16),
        grid=(s // t,),
        in_specs=[pl.BlockSpec((t, width), lambda i: (i, 0)),
                  pl.BlockSpec((SUBLANES, width),
                               lambda i: (jnp.maximum(i * halo_blocks - 1, 0), 0)),
                  pl.BlockSpec((t, width), lambda i: (i, 1)),
                  pl.BlockSpec((CONV_WIDTH, width), const2),
                  pl.BlockSpec((1, width), const2),
                  pl.BlockSpec((n_blocks, LRU_BLOCK_SIZE, LRU_BLOCK_SIZE), const3),
                  pl.BlockSpec((1, width), const2),
                  pl.BlockSpec((n_blocks, LRU_BLOCK_SIZE, LRU_BLOCK_SIZE), const3),
                  pl.BlockSpec((1, width), const2),
                  pl.BlockSpec((1, width), const2)],
        out_specs=pl.BlockSpec((t, width), lambda i: (i, 0)),
        scratch_shapes=[pltpu.VMEM((t, width), F32), pltpu.VMEM((t, width), F32),
                        pltpu.VMEM((t, width), F32), pltpu.VMEM((1, width), F32)],
        compiler_params=_params("arbitrary"),
        name="lru_mixer",
    )(p, p, p, conv_w, row(conv_b), w_a.astype(BF16), row(b_a), w_x.astype(BF16), row(b_x),
      row(lam))


def lru_branch(u, w_in, conv_w, conv_b, w_a, b_a, w_x, b_x, lam, w_out, h_res):
    p = matmul(u, w_in.astype(BF16), F32)
    gated = lru_mixer(p, conv_w, conv_b, w_a, b_a, w_x, b_x, lam)
    return matmul(gated, w_out.astype(BF16), F32, residual=h_res)


def _mlstm_kernel(q_ref, k_ref, v_ref, o_ref, z_ref, gc_ref, gr_ref, bc_ref, br_ref, y_ref,
                  c_sc, n_sc, m_sc):
    n_heads = c_sc.shape[0]
    length = q_ref.shape[0]
    dk = MLSTM_QK_DIM
    dv = MLSTM_V_DIM

    @pl.when(pl.program_id(0) == 0)
    def _():
        c_sc[...] = jnp.zeros_like(c_sc)
        n_sc[...] = jnp.zeros_like(n_sc)
        m_sc[...] = jnp.zeros_like(m_sc)

    gcol = gc_ref[...] + bc_ref[...]
    grow = gr_ref[...] + br_ref[...]
    lane = lax.broadcasted_iota(jnp.int32, gcol.shape, 1)
    is_f_col = (lane >= n_heads) & (lane < 2 * n_heads)
    logf_col = jnp.where(is_f_col, jax.nn.log_sigmoid(gcol), 0.0)
    bcum_col_all = _dot_f32(_lower_tri(length), logf_col)
    logf_row = jax.nn.log_sigmoid(grow[n_heads:, :])
    bcum_row_all = _dot_f32(logf_row, _upper_tri(length))

    row = lax.broadcasted_iota(jnp.int32, (length, length), 0)
    col = lax.broadcasted_iota(jnp.int32, (length, length), 1)
    causal = col <= row

    for h in range(n_heads):
        qh = q_ref[:, h * dk:(h + 1) * dk] * (dk ** -0.5)
        kh = k_ref[:, h * dk:(h + 1) * dk]
        vh = v_ref[:, h * dv:(h + 1) * dv]
        i_col = gcol[:, h:h + 1]
        i_row = grow[h:h + 1, :]
        bcum_col = bcum_col_all[:, n_heads + h:n_heads + h + 1]
        bcum_row = bcum_row_all[h:h + 1, :]
        g = bcum_col[length - 1:length, :]
        m = m_sc[h]
        c_state = c_sc[h]
        n_state = n_sc[h]

        dm = jnp.where(causal, bcum_col - bcum_row + i_row, NEG_INF)
        inter = bcum_col + m
        m_row = jnp.maximum(inter, jnp.max(dm, axis=1, keepdims=True))
        wts = jnp.exp(dm - m_row)
        sc = lax.dot_general(qh, kh, (((1,), (1,)), ((), ())),
                             preferred_element_type=F32) * wts
        inter_w = jnp.exp(inter - m_row)
        cq = lax.dot_general(qh, c_state.astype(BF16), (((1,), (1,)), ((), ())),
                             preferred_element_type=F32)
        num = jnp.dot(sc.astype(BF16), vh.astype(BF16),
                      preferred_element_type=F32) + inter_w * cq
        nq = jnp.sum(qh.astype(F32) * n_state, axis=1, keepdims=True)
        den = jnp.sum(sc, axis=1, keepdims=True) + inter_w * nq
        h_c = num / jnp.maximum(jnp.abs(den), jnp.exp(-m_row))

        a_col = g - bcum_col + i_col
        a_row = g - bcum_row + i_row
        m_new = jnp.maximum(g + m, jnp.max(a_row, axis=1, keepdims=True))
        wk_col = jnp.exp(a_col - m_new)
        wk_row = jnp.exp(a_row - m_new)
        decay = jnp.exp(g + m - m_new)
        c_sc[h] = decay * c_state + lax.dot_general(
            (vh * wk_col).astype(BF16), kh, (((0,), (0,)), ((), ())),
            preferred_element_type=F32)
        n_sc[h] = decay * n_state + jnp.dot(wk_row.astype(BF16), kh,
                                            preferred_element_type=F32)
        m_sc[h] = m_new

        vs = slice(h * dv, (h + 1) * dv)
        y = jax.nn.sigmoid(o_ref[:, vs]) * h_c * jax.nn.silu(z_ref[:, vs])
        y_ref[:, vs] = y.astype(y_ref.dtype)


def mlstm_mixer(qk, voz, gates_col, gates_row, bias_col, bias_row, n_heads):
    s = qk.shape[0]
    length = min(MLSTM_CHUNK, s)
    qk_w = n_heads * MLSTM_QK_DIM
    v_w = n_heads * MLSTM_V_DIM
    return pl.pallas_call(
        _mlstm_kernel,
        out_shape=jax.ShapeDtypeStruct((s, v_w), BF16),
        grid=(s // length,),
        in_specs=[pl.BlockSpec((length, qk_w), lambda c: (c, 0)),
                  pl.BlockSpec((length, qk_w), lambda c: (c, 1)),
                  pl.BlockSpec((length, v_w), lambda c: (c, 0)),
                  pl.BlockSpec((length, v_w), lambda c: (c, 1)),
                  pl.BlockSpec((length, v_w), lambda c: (c, 2)),
                  pl.BlockSpec((length, LANES), lambda c: (c, 0)),
                  pl.BlockSpec((2 * n_heads, length), lambda c: (0, c)),
                  pl.BlockSpec((1, LANES), lambda c: (0, 0)),
                  pl.BlockSpec((2 * n_heads, 1), lambda c: (0, 0))],
        out_specs=pl.BlockSpec((length, v_w), lambda c: (c, 0)),
        scratch_shapes=[pltpu.VMEM((n_heads, MLSTM_V_DIM, MLSTM_QK_DIM), F32),
                        pltpu.VMEM((n_heads, 1, MLSTM_QK_DIM), F32),
                        pltpu.VMEM((n_heads, 1, 1), F32)],
        compiler_params=_params("arbitrary"),
        name="mlstm_mixer",
    )(qk, qk, voz, voz, voz, gates_col, gates_row, bias_row, bias_col)


def mlstm_branch(u, w_in, b_i, b_f, w_out, h_res):
    n_heads = b_i.shape[0]
    qk_w = n_heads * MLSTM_QK_DIM
    v_w = n_heads * MLSTM_V_DIM
    assert 2 * n_heads <= LANES and (2 * n_heads) % SUBLANES == 0
    o_qk = 2 * qk_w
    o_vo = o_qk + 2 * v_w
    o_g = o_vo + 2 * n_heads
    w_qk = w_in[:, :o_qk].astype(BF16)
    w_voz = jnp.concatenate([w_in[:, o_qk:o_vo], w_in[:, o_g:]], axis=1).astype(BF16)
    w_g = jnp.pad(w_in[:, o_vo:o_g], ((0, 0), (0, LANES - 2 * n_heads))).astype(BF16)
    bias = jnp.concatenate([b_i, b_f])
    bias_row = jnp.pad(bias, (0, LANES - 2 * n_heads)).reshape(1, LANES)
    bias_col = bias.reshape(2 * n_heads, 1)
    qk = matmul(u, w_qk, BF16)
    voz = matmul(u, w_voz, F32)
    gates_col = matmul(u, w_g, F32)
    gates_row = gates_col[:, :2 * n_heads].T
    y = mlstm_mixer(qk, voz, gates_col, gates_row, bias_col, bias_row, n_heads)
    return matmul(y, w_out.astype(BF16), F32, residual=h_res)


def kernel(x, norm_g, fox_w_in, fox_b_f, fox_w_out, lru_w_in, lru_conv_w, lru_conv_b, lru_w_a,
           lru_b_a, lru_w_x, lru_b_x, lru_lambda, lru_w_out, mlstm_w_in, mlstm_b_i, mlstm_b_f,
           mlstm_w_out, final_norm_g):
    bsz, s, d = x.shape
    depth = norm_g.shape[0]
    outs = []
    for b in range(bsz):
        h = x[b]
        for i in range(depth):
            j = i // N_MIXERS
            u = rms_norm(h, norm_g[i], BF16)
            kind = i % N_MIXERS
            if kind == 0:
                h = fox_branch(u, fox_w_in[j], fox_b_f[j], fox_w_out[j], h)
            elif kind == 1:
                h = lru_branch(u, lru_w_in[j], lru_conv_w[j], lru_conv_b[j], lru_w_a[j],
                               lru_b_a[j].reshape(-1), lru_w_x[j], lru_b_x[j].reshape(-1),
                               lru_lambda[j], lru_w_out[j], h)
            else:
                h = mlstm_branch(u, mlstm_w_in[j], mlstm_b_i[j], mlstm_b_f[j], mlstm_w_out[j], h)
        outs.append(rms_norm(h, final_norm_g, x.dtype))
    return jnp.stack(outs)
```

```python
import functools
import math

import jax
import jax.numpy as jnp
from jax import lax
from jax.experimental import pallas as pl
from jax.experimental.pallas import tpu as pltpu

F32 = jnp.float32
BF16 = jnp.bfloat16

RMS_EPS = 1e-6
NEG_INF = -1e30
LOG2E = math.log2(math.e)
EXP2_ZERO_BELOW = -160.0
FIXED_STABILISER_MAX_BOUND = 40.0
N_MIXERS = 3

FOX_HEAD_DIM = 128
LRU_BLOCK_SIZE = 256
CONV_WIDTH = 4
LRU_C = 8.0
MLSTM_V_DIM = 512
MLSTM_QK_DIM = 256

LANES = 128
SUBLANES = 8
V7X_VMEM_LIMIT_BYTES = 56 * 1024 * 1024

NORM_ROWS = 256
MM_TM = 1024
MM_TN = 1024
GATE_ROWS = 512
ATTN_TQ = 1024
ATTN_TK = 1024
LRU_ROWS = 256
MLSTM_CHUNK = 256


def _params(*sem):
    return pltpu.CompilerParams(dimension_semantics=sem,
                                vmem_limit_bytes=V7X_VMEM_LIMIT_BYTES)


def _norm_kernel(x_ref, g_ref, o_ref):
    x = x_ref[...]
    y = x * lax.rsqrt(jnp.mean(x * x, axis=-1, keepdims=True) + RMS_EPS)
    o_ref[...] = (y * g_ref[...]).astype(o_ref.dtype)


def rms_norm(x, g, out_dtype):
    s, d = x.shape
    tr = min(NORM_ROWS, s)
    return pl.pallas_call(
        _norm_kernel,
        out_shape=jax.ShapeDtypeStruct((s, d), out_dtype),
        grid=(s // tr,),
        in_specs=[pl.BlockSpec((tr, d), lambda i: (i, 0)),
                  pl.BlockSpec((1, d), lambda i: (0, 0))],
        out_specs=pl.BlockSpec((tr, d), lambda i: (i, 0)),
        compiler_params=_params("parallel"),
        name="rms_norm",
    )(x, g.reshape(1, d))


def _mm_kernel(a_ref, w_ref, o_ref):
    o_ref[...] = jnp.dot(a_ref[...], w_ref[...],
                         preferred_element_type=F32).astype(o_ref.dtype)


def _mm_res_kernel(a_ref, w_ref, r_ref, o_ref):
    o_ref[...] = r_ref[...] + jnp.dot(a_ref[...], w_ref[...],
                                      preferred_element_type=F32)


def matmul(a, w, out_dtype, residual=None):
    m, k = a.shape
    n = w.shape[1]
    tm = min(MM_TM, m)
    tn = min(MM_TN, n)
    in_specs = [pl.BlockSpec((tm, k), lambda i, j: (i, 0)),
                pl.BlockSpec((k, tn), lambda i, j: (0, j))]
    args = [a, w]
    body = _mm_kernel
    if residual is not None:
        in_specs.append(pl.BlockSpec((tm, tn), lambda i, j: (i, j)))
        args.append(residual)
        body = _mm_res_kernel
    return pl.pallas_call(
        body,
        out_shape=jax.ShapeDtypeStruct((m, n), out_dtype),
        grid=(m // tm, n // tn),
        in_specs=in_specs,
        out_specs=pl.BlockSpec((tm, tn), lambda i, j: (i, j)),
        compiler_params=_params("parallel", "parallel"),
        name="matmul_res" if residual is not None else "matmul",
    )(*args)


def _lower_tri(n):
    r = lax.broadcasted_iota(jnp.int32, (n, n), 0)
    c = lax.broadcasted_iota(jnp.int32, (n, n), 1)
    return (c <= r).astype(F32)


def _upper_tri(n):
    r = lax.broadcasted_iota(jnp.int32, (n, n), 0)
    c = lax.broadcasted_iota(jnp.int32, (n, n), 1)
    return (r <= c).astype(F32)


def _dot_f32(a, b):
    return jnp.dot(a, b, precision=lax.Precision.HIGHEST, preferred_element_type=F32)


def _fox_gate_kernel(f_ref, b_ref, fcol_ref, frow_ref, carry_ref):
    @pl.when(pl.program_id(0) == 0)
    def _():
        carry_ref[...] = jnp.zeros_like(carry_ref)

    tb = f_ref.shape[0]
    logf = jax.nn.log_sigmoid(f_ref[...] + b_ref[...])
    cs = _dot_f32(_lower_tri(tb), logf) + carry_ref[...]
    fcol_ref[...] = cs
    frow_ref[...] = cs.T
    carry_ref[...] = cs[tb - 1:tb, :]


def fox_gates(f_pre, b_pad):
    s = f_pre.shape[0]
    tb = min(GATE_ROWS, s)
    return pl.pallas_call(
        _fox_gate_kernel,
        out_shape=(jax.ShapeDtypeStruct((s, LANES), F32),
                   jax.ShapeDtypeStruct((LANES, s), F32)),
        grid=(s // tb,),
        in_specs=[pl.BlockSpec((tb, LANES), lambda i: (i, 0)),
                  pl.BlockSpec((1, LANES), lambda i: (0, 0))],
        out_specs=(pl.BlockSpec((tb, LANES), lambda i: (i, 0)),
                   pl.BlockSpec((LANES, tb), lambda i: (0, i))),
        scratch_shapes=[pltpu.VMEM((1, LANES), F32)],
        compiler_params=_params("arbitrary"),
        name="fox_gates",
    )(f_pre, b_pad)


def _fox_attn_kernel(q_ref, kt_ref, v_ref, z_ref, fcol_ref, frow_ref, o_ref,
                     m_sc, l_sc, acc_sc, kmax_sc, *, scale, tk):
    h = pl.program_id(0)
    i = pl.program_id(1)
    tq = q_ref.shape[0]
    s_len = kt_ref.shape[1]
    c = scale * LOG2E
    reps = tk // LANES
    q = q_ref[...]
    lane = lax.broadcasted_iota(jnp.int32, (tq, LANES), 1)
    fq = jnp.sum(jnp.where(lane == h, fcol_ref[...], 0.0), axis=1, keepdims=True) * LOG2E
    n_full = (i * tq) // tk

    @pl.when(i == 0)
    def _():
        kf = kt_ref[...].astype(F32)
        k_sq = jnp.sum(kf * kf, axis=0, keepdims=True)
        kmax_sc[...] = jnp.sqrt(jnp.max(k_sq, axis=1, keepdims=True))

    qf = q.astype(F32)
    row_bound = c * jnp.sqrt(jnp.sum(qf * qf, axis=1, keepdims=True)) * kmax_sc[...]
    bound = jnp.max(row_bound, axis=0, keepdims=True)
    use_fixed = bound[0, 0] <= FIXED_STABILISER_MAX_BOUND

    threshold = jnp.where(use_fixed, EXP2_ZERO_BELOW, EXP2_ZERO_BELOW - 2.0 * bound)
    fq_max = jnp.max(fq, axis=0, keepdims=True)
    pos = lax.broadcasted_iota(jnp.int32, (1, s_len), 1)
    live = (fq_max - frow_ref[...] * LOG2E) >= threshold
    first_live = jnp.min(jnp.where(live, pos, s_len), axis=1, keepdims=True)
    j_start = jnp.minimum(first_live[0, 0] // tk, n_full)

    l_sc[...] = jnp.zeros_like(l_sc)
    acc_sc[...] = jnp.zeros_like(acc_sc)

    def run(fixed):
        row_bias = jnp.broadcast_to(fq - row_bound if fixed else fq, (tq, LANES))
        if not fixed:
            m_sc[...] = jnp.full_like(m_sc, NEG_INF)

        def block(j, masked):
            off = pl.multiple_of(j * tk, tk)
            kt = kt_ref[:, pl.ds(off, tk)]
            v = v_ref[pl.ds(off, tk), :]
            s = jnp.dot(q, kt, preferred_element_type=F32)
            t = s * c + pltpu.repeat(row_bias, reps, 1) - frow_ref[:, pl.ds(off, tk)] * LOG2E
            if masked:
                row = lax.broadcasted_iota(jnp.int32, (tq, tk), 0) + i * tq
                col = lax.broadcasted_iota(jnp.int32, (tq, tk), 1) + j * tk
                t = jnp.where(col <= row, t, NEG_INF)
            if fixed:
                p = jnp.exp2(t)
                l_sc[...] += jnp.sum(p, axis=1, keepdims=True)
                acc_sc[...] += jnp.dot(p.astype(BF16), v, preferred_element_type=F32)
            else:
                m_prev = m_sc[...]
                m_new = jnp.maximum(m_prev, jnp.max(t, axis=1, keepdims=True))
                alpha = jnp.exp2(m_prev - m_new)
                p = jnp.exp2(t - pltpu.repeat(m_new, reps, 1))
                l_sc[...] = alpha * l_sc[...] + jnp.sum(p, axis=1, keepdims=True)
                acc_sc[...] = alpha * acc_sc[...] + jnp.dot(p.astype(BF16), v,
                                                            preferred_element_type=F32)
                m_sc[...] = m_new

        def below_diagonal(j, carry):
            block(j, False)
            return carry

        lax.fori_loop(j_start, n_full, below_diagonal, 0)
        block(n_full, True)

    pl.when(use_fixed)(functools.partial(run, True))
    pl.when(jnp.logical_not(use_fixed))(functools.partial(run, False))

    o = acc_sc[...] / l_sc[...]
    o_ref[...] = (o * jax.nn.silu(z_ref[...])).astype(o_ref.dtype)


def fox_attention(qkv, k_t, z, fcol, frow3, n_heads):
    s = qkv.shape[0]
    dh = FOX_HEAD_DIM
    tq = min(ATTN_TQ, s)
    tk = min(ATTN_TK, s)
    assert tk % tq == 0 and tk % LANES == 0
    kernel = functools.partial(_fox_attn_kernel, scale=dh ** -0.5, tk=tk)
    return pl.pallas_call(
        kernel,
        out_shape=jax.ShapeDtypeStruct((s, n_heads * dh), BF16),
        grid=(n_heads, s // tq),
        in_specs=[pl.BlockSpec((tq, dh), lambda h, i: (i, h)),
                  pl.BlockSpec((dh, s), lambda h, i: (h, 0)),
                  pl.BlockSpec((s, dh), lambda h, i: (0, 2 * n_heads + h)),
                  pl.BlockSpec((tq, dh), lambda h, i: (i, h)),
                  pl.BlockSpec((tq, LANES), lambda h, i: (i, 0)),
                  pl.BlockSpec((None, 1, s), lambda h, i: (h, 0, 0))],
        out_specs=pl.BlockSpec((tq, dh), lambda h, i: (i, h)),
        scratch_shapes=[pltpu.VMEM((tq, LANES), F32), pltpu.VMEM((tq, LANES), F32),
                        pltpu.VMEM((tq, dh), F32), pltpu.VMEM((1, 1), F32)],
        compiler_params=_params("parallel", "arbitrary"),
        name="fox_attention",
    )(qkv, k_t, qkv, z, fcol, frow3)


def fox_branch(u, w_in, b_f, w_out, h_res):
    s, d = u.shape
    n_heads = b_f.shape[0]
    width = n_heads * FOX_HEAD_DIM
    assert n_heads <= LANES
    w_qkv = w_in[:, :3 * width].astype(BF16)
    w_z = w_in[:, 3 * width:4 * width].astype(BF16)
    w_f = jnp.pad(w_in[:, 4 * width:], ((0, 0), (0, LANES - n_heads))).astype(BF16)
    b_pad = jnp.pad(b_f, (0, LANES - n_heads)).reshape(1, LANES)
    qkv = matmul(u, w_qkv, BF16)
    z = matmul(u, w_z, F32)
    f_pre = matmul(u, w_f, F32)
    fcol, frow = fox_gates(f_pre, b_pad)
    frow3 = frow[:n_heads].reshape(n_heads, 1, s)
    k_t = qkv[:, width:2 * width].T
    gated = fox_attention(qkv, k_t, z, fcol, frow3, n_heads)
    return matmul(gated, w_out.astype(BF16), F32, residual=h_res)


def _lru_kernel(x_ref, halo_ref, z_ref, cw_ref, cb_ref, wa_ref, ba_ref, wx_ref, bx_ref,
                lam_ref, o_ref, a_sc, b_sc, h_sc, state_sc):
    i = pl.program_id(0)
    t, width = x_ref.shape
    n_blocks = wa_ref.shape[0]
    bs = LRU_BLOCK_SIZE

    @pl.when(i == 0)
    def _():
        state_sc[...] = jnp.zeros_like(state_sc)

    halo = jnp.where(i == 0, 0.0, halo_ref[...])
    ext = jnp.concatenate([halo, x_ref[...]], axis=0)
    xc = cb_ref[...] + x_ref[...] * cw_ref[CONV_WIDTH - 1:CONV_WIDTH, :]
    for j in range(CONV_WIDTH - 1):
        shifted = pltpu.roll(ext, CONV_WIDTH - 1 - j, axis=0)[SUBLANES:, :]
        xc = xc + shifted * cw_ref[j:j + 1, :]

    for n in range(n_blocks):
        sl = slice(n * bs, (n + 1) * bs)
        xn = xc[:, sl]
        xn16 = xn.astype(BF16)
        r = jax.nn.sigmoid(jnp.dot(xn16, wa_ref[n], preferred_element_type=F32) + ba_ref[:, sl])
        ig = jax.nn.sigmoid(jnp.dot(xn16, wx_ref[n], preferred_element_type=F32) + bx_ref[:, sl])
        log_a = -LRU_C * r * jax.nn.softplus(-lam_ref[:, sl])
        a = jnp.exp(log_a)
        a_sc[:, sl] = a
        b_sc[:, sl] = jnp.sqrt(-jnp.tanh(log_a) * (a * a + 1.0)) * (ig * xn)

    def step(r, hprev):
        hnew = a_sc[pl.ds(r, 1), :] * hprev + b_sc[pl.ds(r, 1), :]
        h_sc[pl.ds(r, 1), :] = hnew
        return hnew

    state_sc[...] = lax.fori_loop(0, t, step, state_sc[...], unroll=8)
    o_ref[...] = (h_sc[...] * jax.nn.silu(z_ref[...])).astype(o_ref.dtype)


def lru_mixer(p, conv_w, conv_b, w_a, b_a, w_x, b_x, lam):
    s = p.shape[0]
    width = p.shape[1] // 2
    n_blocks = w_a.shape[0]
    t = min(LRU_ROWS, s)
    halo_blocks = t // SUBLANES
    row = lambda v: v.reshape(1, width)
    const2 = lambda i: (0, 0)
    const3 = lambda i: (0, 0, 0)
    return pl.pallas_call(
        _lru_kernel,
        out_shape=jax.ShapeDtypeStruct((s, width), BF16),
        grid=(s // t,),
        in_specs=[pl.BlockSpec((t, width), lambda i: (i, 0)),
                  pl.BlockSpec((SUBLANES, width),
                               lambda i: (jnp.maximum(i * halo_blocks - 1, 0), 0)),
                  pl.BlockSpec((t, width), lambda i: (i, 1)),
                  pl.BlockSpec((CONV_WIDTH, width), const2),
                  pl.BlockSpec((1, width), const2),
                  pl.BlockSpec((n_blocks, LRU_BLOCK_SIZE, LRU_BLOCK_SIZE), const3),
                  pl.BlockSpec((1, width), const2),
                  pl.BlockSpec((n_blocks, LRU_BLOCK_SIZE, LRU_BLOCK_SIZE), const3),
                  pl.BlockSpec((1, width), const2),
                  pl.BlockSpec((1, width), const2)],
        out_specs=pl.BlockSpec((t, width), lambda i: (i, 0)),
        scratch_shapes=[pltpu.VMEM((t, width), F32), pltpu.VMEM((t, width), F32),
                        pltpu.VMEM((t, width), F32), pltpu.VMEM((1, width), F32)],
        compiler_params=_params("arbitrary"),
        name="lru_mixer",
    )(p, p, p, conv_w, row(conv_b), w_a.astype(BF16), row(b_a), w_x.astype(BF16), row(b_x),
      row(lam))


def lru_branch(u, w_in, conv_w, conv_b, w_a, b_a, w_x, b_x, lam, w_out, h_res):
    p = matmul(u, w_in.astype(BF16), F32)
    gated = lru_mixer(p, conv_w, conv_b, w_a, b_a, w_x, b_x, lam)
    return matmul(gated, w_out.astype(BF16), F32, residual=h_res)


def _mlstm_kernel(q_ref, k_ref, v_ref, o_ref, z_ref, gc_ref, gr_ref, bc_ref, br_ref, y_ref,
                  c_sc, n_sc, m_sc):
    n_heads = c_sc.shape[0]
    length = q_ref.shape[0]
    dk = MLSTM_QK_DIM
    dv = MLSTM_V_DIM

    @pl.when(pl.program_id(0) == 0)
    def _():
        c_sc[...] = jnp.zeros_like(c_sc)
        n_sc[...] = jnp.zeros_like(n_sc)
        m_sc[...] = jnp.zeros_like(m_sc)

    gcol = gc_ref[...] + bc_ref[...]
    grow = gr_ref[...] + br_ref[...]
    lane = lax.broadcasted_iota(jnp.int32, gcol.shape, 1)
    is_f_col = (lane >= n_heads) & (lane < 2 * n_heads)
    logf_col = jnp.where(is_f_col, jax.nn.log_sigmoid(gcol), 0.0)
    bcum_col_all = _dot_f32(_lower_tri(length), logf_col)
    logf_row = jax.nn.log_sigmoid(grow[n_heads:, :])
    bcum_row_all = _dot_f32(logf_row, _upper_tri(length))

    row = lax.broadcasted_iota(jnp.int32, (length, length), 0)
    col = lax.broadcasted_iota(jnp.int32, (length, length), 1)
    causal = col <= row

    for h in range(n_heads):
        qh = q_ref[:, h * dk:(h + 1) * dk] * (dk ** -0.5)
        kh = k_ref[:, h * dk:(h + 1) * dk]
        vh = v_ref[:, h * dv:(h + 1) * dv]
        i_col = gcol[:, h:h + 1]
        i_row = grow[h:h + 1, :]
        bcum_col = bcum_col_all[:, n_heads + h:n_heads + h + 1]
        bcum_row = bcum_row_all[h:h + 1, :]
        g = bcum_col[length - 1:length, :]
        m = m_sc[h]
        c_state = c_sc[h]
        n_state = n_sc[h]

        dm = jnp.where(causal, bcum_col - bcum_row + i_row, NEG_INF)
        inter = bcum_col + m
        m_row = jnp.maximum(inter, jnp.max(dm, axis=1, keepdims=True))
        wts = jnp.exp(dm - m_row)
        sc = lax.dot_general(qh, kh, (((1,), (1,)), ((), ())),
                             preferred_element_type=F32) * wts
        inter_w = jnp.exp(inter - m_row)
        cq = lax.dot_general(qh, c_state.astype(BF16), (((1,), (1,)), ((), ())),
                             preferred_element_type=F32)
        num = jnp.dot(sc.astype(BF16), vh.astype(BF16),
                      preferred_element_type=F32) + inter_w * cq
        nq = jnp.sum(qh.astype(F32) * n_state, axis=1, keepdims=True)
        den = jnp.sum(sc, axis=1, keepdims=True) + inter_w * nq
        h_c = num / jnp.maximum(jnp.abs(den), jnp.exp(-m_row))

        a_col = g - bcum_col + i_col
        a_row = g - bcum_row + i_row
        m_new = jnp.maximum(g + m, jnp.max(a_row, axis=1, keepdims=True))
        wk_col = jnp.exp(a_col - m_new)
        wk_row = jnp.exp(a_row - m_new)
        decay = jnp.exp(g + m - m_new)
        c_sc[h] = decay * c_state + lax.dot_general(
            (vh * wk_col).astype(BF16), kh, (((0,), (0,)), ((), ())),
            preferred_element_type=F32)
        n_sc[h] = decay * n_state + jnp.dot(wk_row.astype(BF16), kh,
                                            preferred_element_type=F32)
        m_sc[h] = m_new

        vs = slice(h * dv, (h + 1) * dv)
        y = jax.nn.sigmoid(o_ref[:, vs]) * h_c * jax.nn.silu(z_ref[:, vs])
        y_ref[:, vs] = y.astype(y_ref.dtype)


def mlstm_mixer(qk, voz, gates_col, gates_row, bias_col, bias_row, n_heads):
    s = qk.shape[0]
    length = min(MLSTM_CHUNK, s)
    qk_w = n_heads * MLSTM_QK_DIM
    v_w = n_heads * MLSTM_V_DIM
    return pl.pallas_call(
        _mlstm_kernel,
        out_shape=jax.ShapeDtypeStruct((s, v_w), BF16),
        grid=(s // length,),
        in_specs=[pl.BlockSpec((length, qk_w), lambda c: (c, 0)),
                  pl.BlockSpec((length, qk_w), lambda c: (c, 1)),
                  pl.BlockSpec((length, v_w), lambda c: (c, 0)),
                  pl.BlockSpec((length, v_w), lambda c: (c, 1)),
                  pl.BlockSpec((length, v_w), lambda c: (c, 2)),
                  pl.BlockSpec((length, LANES), lambda c: (c, 0)),
                  pl.BlockSpec((2 * n_heads, length), lambda c: (0, c)),
                  pl.BlockSpec((1, LANES), lambda c: (0, 0)),
                  pl.BlockSpec((2 * n_heads, 1), lambda c: (0, 0))],
        out_specs=pl.BlockSpec((length, v_w), lambda c: (c, 0)),
        scratch_shapes=[pltpu.VMEM((n_heads, MLSTM_V_DIM, MLSTM_QK_DIM), F32),
                        pltpu.VMEM((n_heads, 1, MLSTM_QK_DIM), F32),
                        pltpu.VMEM((n_heads, 1, 1), F32)],
        compiler_params=_params("arbitrary"),
        name="mlstm_mixer",
    )(qk, qk, voz, voz, voz, gates_col, gates_row, bias_row, bias_col)


def mlstm_branch(u, w_in, b_i, b_f, w_out, h_res):
    n_heads = b_i.shape[0]
    qk_w = n_heads * MLSTM_QK_DIM
    v_w = n_heads * MLSTM_V_DIM
    assert 2 * n_heads <= LANES and (2 * n_heads) % SUBLANES == 0
    o_qk = 2 * qk_w
    o_vo = o_qk + 2 * v_w
    o_g = o_vo + 2 * n_heads
    w_qk = w_in[:, :o_qk].astype(BF16)
    w_voz = jnp.concatenate([w_in[:, o_qk:o_vo], w_in[:, o_g:]], axis=1).astype(BF16)
    w_g = jnp.pad(w_in[:, o_vo:o_g], ((0, 0), (0, LANES - 2 * n_heads))).astype(BF16)
    bias = jnp.concatenate([b_i, b_f])
    bias_row = jnp.pad(bias, (0, LANES - 2 * n_heads)).reshape(1, LANES)
    bias_col = bias.reshape(2 * n_heads, 1)
    qk = matmul(u, w_qk, BF16)
    voz = matmul(u, w_voz, F32)
    gates_col = matmul(u, w_g, F32)
    gates_row = gates_col[:, :2 * n_heads].T
    y = mlstm_mixer(qk, voz, gates_col, gates_row, bias_col, bias_row, n_heads)
    return matmul(y, w_out.astype(BF16), F32, residual=h_res)


def kernel(x, norm_g, fox_w_in, fox_b_f, fox_w_out, lru_w_in, lru_conv_w, lru_conv_b, lru_w_a,
           lru_b_a, lru_w_x, lru_b_x, lru_lambda, lru_w_out, mlstm_w_in, mlstm_b_i, mlstm_b_f,
           mlstm_w_out, final_norm_g):
    bsz, s, d = x.shape
    depth = norm_g.shape[0]
    outs = []
    for b in range(bsz):
        h = x[b]
        for i in range(depth):
            j = i // N_MIXERS
            u = rms_norm(h, norm_g[i], BF16)
            kind = i % N_MIXERS
            if kind == 0:
                h = fox_branch(u, fox_w_in[j], fox_b_f[j], fox_w_out[j], h)
            elif kind == 1:
                h = lru_branch(u, lru_w_in[j], lru_conv_w[j], lru_conv_b[j], lru_w_a[j],
                               lru_b_a[j].reshape(-1), lru_w_x[j], lru_b_x[j].reshape(-1),
                               lru_lambda[j], lru_w_out[j], h)
            else:
                h = mlstm_branch(u, mlstm_w_in[j], mlstm_b_i[j], mlstm_b_f[j], mlstm_w_out[j], h)
        outs.append(rms_norm(h, final_norm_g, x.dtype))
    return jnp.stack(outs)
```

```python
import functools
import math

import jax
import jax.numpy as jnp
from jax import lax
from jax.experimental import pallas as pl
from jax.experimental.pallas import tpu as pltpu

F32 = jnp.float32
BF16 = jnp.bfloat16

RMS_EPS = 1e-6
NEG_INF = -1e30
LOG2E = math.log2(math.e)
EXP2_ZERO_BELOW = -160.0
FIXED_STABILISER_MAX_BOUND = 40.0
AUG_ROWS = 16
N_MIXERS = 3

FOX_HEAD_DIM = 128
LRU_BLOCK_SIZE = 256
CONV_WIDTH = 4
LRU_C = 8.0
MLSTM_V_DIM = 512
MLSTM_QK_DIM = 256

LANES = 128
SUBLANES = 8
V7X_VMEM_LIMIT_BYTES = 56 * 1024 * 1024

NORM_ROWS = 256
MM_TM = 1024
MM_TN = 1024
GATE_ROWS = 512
ATTN_TQ = 1024
ATTN_TK = 1024
LRU_ROWS = 256
MLSTM_CHUNK = 256


def _params(*sem):
    return pltpu.CompilerParams(dimension_semantics=sem,
                                vmem_limit_bytes=V7X_VMEM_LIMIT_BYTES)


def _norm_kernel(x_ref, g_ref, o_ref):
    x = x_ref[...]
    y = x * lax.rsqrt(jnp.mean(x * x, axis=-1, keepdims=True) + RMS_EPS)
    o_ref[...] = (y * g_ref[...]).astype(o_ref.dtype)


def rms_norm(x, g, out_dtype):
    s, d = x.shape
    tr = min(NORM_ROWS, s)
    return pl.pallas_call(
        _norm_kernel,
        out_shape=jax.ShapeDtypeStruct((s, d), out_dtype),
        grid=(s // tr,),
        in_specs=[pl.BlockSpec((tr, d), lambda i: (i, 0)),
                  pl.BlockSpec((1, d), lambda i: (0, 0))],
        out_specs=pl.BlockSpec((tr, d), lambda i: (i, 0)),
        compiler_params=_params("parallel"),
        name="rms_norm",
    )(x, g.reshape(1, d))


def _mm_kernel(a_ref, w_ref, o_ref):
    o_ref[...] = jnp.dot(a_ref[...], w_ref[...],
                         preferred_element_type=F32).astype(o_ref.dtype)


def _mm_res_kernel(a_ref, w_ref, r_ref, o_ref):
    o_ref[...] = r_ref[...] + jnp.dot(a_ref[...], w_ref[...],
                                      preferred_element_type=F32)


def _col_window(w, cols, tn_max):
    start, n = (0, w.shape[1]) if cols is None else cols
    tn = min(tn_max, n)
    assert start % tn == 0 and n % tn == 0
    return start // tn, n, tn


def matmul(a, w, out_dtype, residual=None, cols=None):
    m, k = a.shape
    j0, n, tn = _col_window(w, cols, MM_TN)
    tm = min(MM_TM, m)
    in_specs = [pl.BlockSpec((tm, k), lambda i, j: (i, 0)),
                pl.BlockSpec((k, tn), lambda i, j: (0, j + j0))]
    args = [a, w]
    body = _mm_kernel
    if residual is not None:
        in_specs.append(pl.BlockSpec((tm, tn), lambda i, j: (i, j)))
        args.append(residual)
        body = _mm_res_kernel
    return pl.pallas_call(
        body,
        out_shape=jax.ShapeDtypeStruct((m, n), out_dtype),
        grid=(m // tm, n // tn),
        in_specs=in_specs,
        out_specs=pl.BlockSpec((tm, tn), lambda i, j: (i, j)),
        compiler_params=_params("parallel", "parallel"),
        name="matmul_res" if residual is not None else "matmul",
    )(*args)


def _mm_t_kernel(a_ref, w_ref, o_ref):
    o_ref[...] = lax.dot_general(w_ref[...], a_ref[...], (((0,), (1,)), ((), ())),
                                 preferred_element_type=F32).astype(o_ref.dtype)


def matmul_t(a, w, out_dtype, cols=None):
    m, k = a.shape
    j0, n, tn = _col_window(w, cols, MM_TN)
    tm = min(MM_TM, m)
    return pl.pallas_call(
        _mm_t_kernel,
        out_shape=jax.ShapeDtypeStruct((n, m), out_dtype),
        grid=(m // tm, n // tn),
        in_specs=[pl.BlockSpec((tm, k), lambda i, j: (i, 0)),
                  pl.BlockSpec((k, tn), lambda i, j: (0, j + j0))],
        out_specs=pl.BlockSpec((tn, tm), lambda i, j: (j, i)),
        compiler_params=_params("parallel", "parallel"),
        name="matmul_t",
    )(a, w)


def _lower_tri(n):
    r = lax.broadcasted_iota(jnp.int32, (n, n), 0)
    c = lax.broadcasted_iota(jnp.int32, (n, n), 1)
    return (c <= r).astype(F32)


def _upper_tri(n):
    r = lax.broadcasted_iota(jnp.int32, (n, n), 0)
    c = lax.broadcasted_iota(jnp.int32, (n, n), 1)
    return (r <= c).astype(F32)


def _dot_f32(a, b):
    return jnp.dot(a, b, precision=lax.Precision.HIGHEST, preferred_element_type=F32)


def _fox_gate_kernel(f_ref, b_ref, fcol_ref, frow_ref, carry_ref):
    @pl.when(pl.program_id(0) == 0)
    def _():
        carry_ref[...] = jnp.zeros_like(carry_ref)

    tb = f_ref.shape[0]
    logf = jax.nn.log_sigmoid(f_ref[...] + b_ref[...])
    cs = _dot_f32(_lower_tri(tb), logf) + carry_ref[...]
    fcol_ref[...] = cs
    frow_ref[...] = cs.T
    carry_ref[...] = cs[tb - 1:tb, :]


def fox_gates(f_pre, b_pad):
    s = f_pre.shape[0]
    tb = min(GATE_ROWS, s)
    return pl.pallas_call(
        _fox_gate_kernel,
        out_shape=(jax.ShapeDtypeStruct((s, LANES), F32),
                   jax.ShapeDtypeStruct((LANES, s), F32)),
        grid=(s // tb,),
        in_specs=[pl.BlockSpec((tb, LANES), lambda i: (i, 0)),
                  pl.BlockSpec((1, LANES), lambda i: (0, 0))],
        out_specs=(pl.BlockSpec((tb, LANES), lambda i: (i, 0)),
                   pl.BlockSpec((LANES, tb), lambda i: (0, i))),
        scratch_shapes=[pltpu.VMEM((1, LANES), F32)],
        compiler_params=_params("arbitrary"),
        name="fox_gates",
    )(f_pre, b_pad)


def _split3(x):
    hi = x.astype(BF16)
    rest = x - hi.astype(F32)
    mid = rest.astype(BF16)
    lo = (rest - mid.astype(F32)).astype(BF16)
    return hi.astype(F32), mid.astype(F32), lo.astype(F32)


def _fox_attn_kernel(q_ref, kt_ref, v_ref, z_ref, fcol_ref, frow_ref, o_ref,
                     m_sc, l_sc, acc_sc, kmax_sc, qaug_sc, kaug_sc, *, scale, tk):
    h = pl.program_id(0)
    i = pl.program_id(1)
    tq, dh = q_ref.shape
    s_len = kt_ref.shape[1]
    c = scale * LOG2E
    reps = tk // LANES
    q = q_ref[...]
    lane = lax.broadcasted_iota(jnp.int32, (tq, LANES), 1)
    fq = jnp.sum(jnp.where(lane == h, fcol_ref[...], 0.0), axis=1, keepdims=True) * LOG2E
    n_full = (i * tq) // tk

    @pl.when(i == 0)
    def _():
        kf = kt_ref[...].astype(F32)
        k_sq = jnp.sum(kf * kf, axis=0, keepdims=True)
        kmax_sc[...] = jnp.sqrt(jnp.max(k_sq, axis=1, keepdims=True))
        hi, mid, lo = _split3(frow_ref[...] * (-1.0 / scale))
        sub = lax.broadcasted_iota(jnp.int32, (AUG_ROWS, s_len), 0)
        extra = jnp.where(sub < 3, 1.0,
                          jnp.where(sub == 3, hi, jnp.where(sub == 4, mid,
                                                            jnp.where(sub == 5, lo, 0.0))))
        kaug_sc[:dh, :] = kt_ref[...]
        kaug_sc[dh:dh + AUG_ROWS, :] = extra.astype(BF16)
        kaug_sc[dh + AUG_ROWS:, :] = jnp.zeros((dh - AUG_ROWS, s_len), BF16)

    qf = q.astype(F32)
    row_bound = c * jnp.sqrt(jnp.sum(qf * qf, axis=1, keepdims=True)) * kmax_sc[...]
    bound = jnp.max(row_bound, axis=0, keepdims=True)
    use_fixed = bound[0, 0] <= FIXED_STABILISER_MAX_BOUND

    threshold = jnp.where(use_fixed, EXP2_ZERO_BELOW, EXP2_ZERO_BELOW - 2.0 * bound)
    fq_max = jnp.max(fq, axis=0, keepdims=True)
    pos = lax.broadcasted_iota(jnp.int32, (1, s_len), 1)
    live = (fq_max - frow_ref[...] * LOG2E) >= threshold
    first_live = jnp.min(jnp.where(live, pos, s_len), axis=1, keepdims=True)
    j_start = jnp.minimum(first_live[0, 0] // tk, n_full)

    l_sc[...] = jnp.zeros_like(l_sc)
    acc_sc[...] = jnp.zeros_like(acc_sc)
    qaug_sc[:, :dh] = q

    def run(fixed):
        hi, mid, lo = _split3((fq - row_bound if fixed else fq) * (1.0 / c))
        extra = jnp.where(lane == 0, hi, jnp.where(lane == 1, mid, jnp.where(
            lane == 2, lo, jnp.where(lane < 6, 1.0, 0.0))))
        qaug_sc[:, dh:] = extra.astype(BF16)
        if not fixed:
            m_sc[...] = jnp.full_like(m_sc, NEG_INF)

        def block(j, masked):
            off = pl.multiple_of(j * tk, tk)
            v = v_ref[pl.ds(off, tk), :]
            s = jnp.dot(qaug_sc[...], kaug_sc[:, pl.ds(off, tk)], preferred_element_type=F32)
            t = s * c
            if masked:
                row = lax.broadcasted_iota(jnp.int32, (tq, tk), 0) + i * tq
                col = lax.broadcasted_iota(jnp.int32, (tq, tk), 1) + j * tk
                t = jnp.where(col <= row, t, NEG_INF)
            if fixed:
                p = jnp.exp2(t)
                l_sc[...] += jnp.sum(p, axis=1, keepdims=True)
                acc_sc[...] += jnp.dot(p.astype(BF16), v, preferred_element_type=F32)
            else:
                m_prev = m_sc[...]
                m_new = jnp.maximum(m_prev, jnp.max(t, axis=1, keepdims=True))
                alpha = jnp.exp2(m_prev - m_new)
                p = jnp.exp2(t - jnp.tile(m_new, (1, reps)))
                l_sc[...] = alpha * l_sc[...] + jnp.sum(p, axis=1, keepdims=True)
                acc_sc[...] = alpha * acc_sc[...] + jnp.dot(p.astype(BF16), v,
                                                            preferred_element_type=F32)
                m_sc[...] = m_new

        def below_diagonal(j, carry):
            block(j, False)
            return carry

        lax.fori_loop(j_start, n_full, below_diagonal, 0)
        block(n_full, True)

    pl.when(use_fixed)(functools.partial(run, True))
    pl.when(jnp.logical_not(use_fixed))(functools.partial(run, False))

    o = acc_sc[...] / l_sc[...]
    o_ref[...] = (o * jax.nn.silu(z_ref[...])).astype(o_ref.dtype)


def fox_attention(q, k_t, v, z, fcol, frow3, n_heads):
    s = q.shape[0]
    dh = FOX_HEAD_DIM
    tq = min(ATTN_TQ, s)
    tk = min(ATTN_TK, s)
    assert tk % tq == 0 and tk % LANES == 0 and dh == LANES
    kernel = functools.partial(_fox_attn_kernel, scale=dh ** -0.5, tk=tk)
    return pl.pallas_call(
        kernel,
        out_shape=jax.ShapeDtypeStruct((s, n_heads * dh), BF16),
        grid=(n_heads, s // tq),
        in_specs=[pl.BlockSpec((tq, dh), lambda h, i: (i, h)),
                  pl.BlockSpec((dh, s), lambda h, i: (h, 0)),
                  pl.BlockSpec((s, dh), lambda h, i: (0, h)),
                  pl.BlockSpec((tq, dh), lambda h, i: (i, h)),
                  pl.BlockSpec((tq, LANES), lambda h, i: (i, 0)),
                  pl.BlockSpec((None, 1, s), lambda h, i: (h, 0, 0))],
        out_specs=pl.BlockSpec((tq, dh), lambda h, i: (i, h)),
        scratch_shapes=[pltpu.VMEM((tq, LANES), F32), pltpu.VMEM((tq, LANES), F32),
                        pltpu.VMEM((tq, dh), F32), pltpu.VMEM((1, 1), F32),
                        pltpu.VMEM((tq, 2 * dh), BF16), pltpu.VMEM((2 * dh, s), BF16)],
        compiler_params=_params("parallel", "arbitrary"),
        name="fox_attention",
    )(q, k_t, v, z, fcol, frow3)


def fox_branch(u, w_in, b_f, w_out, h_res):
    s, d = u.shape
    n_heads = b_f.shape[0]
    width = n_heads * FOX_HEAD_DIM
    assert n_heads <= LANES
    w16 = w_in.astype(BF16)
    w_f = jnp.pad(w_in[:, 4 * width:], ((0, 0), (0, LANES - n_heads))).astype(BF16)
    b_pad = jnp.pad(b_f, (0, LANES - n_heads)).reshape(1, LANES)
    q = matmul(u, w16, BF16, cols=(0, width))
    k_t = matmul_t(u, w16, BF16, cols=(width, width))
    v = matmul(u, w16, BF16, cols=(2 * width, width))
    z = matmul(u, w16, F32, cols=(3 * width, width))
    f_pre = matmul(u, w_f, F32)
    fcol, frow = fox_gates(f_pre, b_pad)
    frow3 = frow[:n_heads].reshape(n_heads, 1, s)
    gated = fox_attention(q, k_t, v, z, fcol, frow3, n_heads)
    return matmul(gated, w_out.astype(BF16), F32, residual=h_res)


def _lru_kernel(x_ref, halo_ref, z_ref, cw_ref, cb_ref, wa_ref, ba_ref, wx_ref, bx_ref,
                lam_ref, o_ref, a_sc, b_sc, h_sc, state_sc):
    i = pl.program_id(0)
    t, width = x_ref.shape
    n_blocks = wa_ref.shape[0]
    bs = LRU_BLOCK_SIZE

    @pl.when(i == 0)
    def _():
        state_sc[...] = jnp.zeros_like(state_sc)

    halo = jnp.where(i == 0, 0.0, halo_ref[...])
    ext = jnp.concatenate([halo, x_ref[...]], axis=0)
    xc = cb_ref[...] + x_ref[...] * cw_ref[CONV_WIDTH - 1:CONV_WIDTH, :]
    for j in range(CONV_WIDTH - 1):
        shifted = pltpu.roll(ext, CONV_WIDTH - 1 - j, axis=0)[SUBLANES:, :]
        xc = xc + shifted * cw_ref[j:j + 1, :]

    for n in range(n_blocks):
        sl = slice(n * bs, (n + 1) * bs)
        xn = xc[:, sl]
        xn16 = xn.astype(BF16)
        r = jax.nn.sigmoid(jnp.dot(xn16, wa_ref[n], preferred_element_type=F32) + ba_ref[:, sl])
        ig = jax.nn.sigmoid(jnp.dot(xn16, wx_ref[n], preferred_element_type=F32) + bx_ref[:, sl])
        log_a = -LRU_C * r * jax.nn.softplus(-lam_ref[:, sl])
        a = jnp.exp(log_a)
        a_sc[:, sl] = a
        b_sc[:, sl] = jnp.sqrt(-jnp.tanh(log_a) * (a * a + 1.0)) * (ig * xn)

    def step(r, hprev):
        hnew = a_sc[pl.ds(r, 1), :] * hprev + b_sc[pl.ds(r, 1), :]
        h_sc[pl.ds(r, 1), :] = hnew
        return hnew

    state_sc[...] = lax.fori_loop(0, t, step, state_sc[...], unroll=8)
    o_ref[...] = (h_sc[...] * jax.nn.silu(z_ref[...])).astype(o_ref.dtype)


def lru_mixer(p, conv_w, conv_b, w_a, b_a, w_x, b_x, lam):
    s = p.shape[0]
    width = p.shape[1] // 2
    n_blocks = w_a.shape[0]
    t = min(LRU_ROWS, s)
    halo_blocks = t // SUBLANES
    row = lambda v: v.reshape(1, width)
    const2 = lambda i: (0, 0)
    const3 = lambda i: (0, 0, 0)
    return pl.pallas_call(
        _lru_kernel,
        out_shape=jax.ShapeDtypeStruct((s, width), BF16),
        grid=(s // t,),
        in_specs=[pl.BlockSpec((t, width), lambda i: (i, 0)),
                  pl.BlockSpec((SUBLANES, width),
                               lambda i: (jnp.maximum(i * halo_blocks - 1, 0), 0)),
                  pl.BlockSpec((t, width), lambda i: (i, 1)),
                  pl.BlockSpec((CONV_WIDTH, width), const2),
                  pl.BlockSpec((1, width), const2),
                  pl.BlockSpec((n_blocks, LRU_BLOCK_SIZE, LRU_BLOCK_SIZE), const3),
                  pl.BlockSpec((1, width), const2),
                  pl.BlockSpec((n_blocks, LRU_BLOCK_SIZE, LRU_BLOCK_SIZE), const3),
                  pl.BlockSpec((1, width), const2),
                  pl.BlockSpec((1, width), const2)],
        out_specs=pl.BlockSpec((t, width), lambda i: (i, 0)),
        scratch_shapes=[pltpu.VMEM((t, width), F32), pltpu.VMEM((t, width), F32),
                        pltpu.VMEM((t, width), F32), pltpu.VMEM((1, width), F32)],
        compiler_params=_params("arbitrary"),
        name="lru_mixer",
    )(p, p, p, conv_w, row(conv_b), w_a.astype(BF16), row(b_a), w_x.astype(BF16), row(b_x),
      row(lam))


def lru_branch(u, w_in, conv_w, conv_b, w_a, b_a, w_x, b_x, lam, w_out, h_res):
    p = matmul(u, w_in.astype(BF16), F32)
    gated = lru_mixer(p, conv_w, conv_b, w_a, b_a, w_x, b_x, lam)
    return matmul(gated, w_out.astype(BF16), F32, residual=h_res)


def _mlstm_kernel(q_ref, k_ref, v_ref, o_ref, z_ref, gc_ref, gr_ref, bc_ref, br_ref, y_ref,
                  c_sc, n_sc, m_sc):
    n_heads = c_sc.shape[0]
    length = q_ref.shape[0]
    dk = MLSTM_QK_DIM
    dv = MLSTM_V_DIM

    @pl.when(pl.program_id(0) == 0)
    def _():
        c_sc[...] = jnp.zeros_like(c_sc)
        n_sc[...] = jnp.zeros_like(n_sc)
        m_sc[...] = jnp.zeros_like(m_sc)

    gcol = gc_ref[...] + bc_ref[...]
    grow = gr_ref[...] + br_ref[...]
    lane = lax.broadcasted_iota(jnp.int32, gcol.shape, 1)
    is_f_col = (lane >= n_heads) & (lane < 2 * n_heads)
    logf_col = jnp.where(is_f_col, jax.nn.log_sigmoid(gcol), 0.0)
    bcum_col_all = _dot_f32(_lower_tri(length), logf_col)
    logf_row = jax.nn.log_sigmoid(grow[n_heads:, :])
    bcum_row_all = _dot_f32(logf_row, _upper_tri(length))

    row = lax.broadcasted_iota(jnp.int32, (length, length), 0)
    col = lax.broadcasted_iota(jnp.int32, (length, length), 1)
    causal = col <= row

    for h in range(n_heads):
        qh = q_ref[:, h * dk:(h + 1) * dk] * (dk ** -0.5)
        kh = k_ref[:, h * dk:(h + 1) * dk]
        vh = v_ref[:, h * dv:(h + 1) * dv]
        i_col = gcol[:, h:h + 1]
        i_row = grow[h:h + 1, :]
        bcum_col = bcum_col_all[:, n_heads + h:n_heads + h + 1]
        bcum_row = bcum_row_all[h:h + 1, :]
        g = bcum_col[length - 1:length, :]
        m = m_sc[h]
        c_state = c_sc[h]
        n_state = n_sc[h]

        dm = jnp.where(causal, bcum_col - bcum_row + i_row, NEG_INF)
        inter = bcum_col + m
        m_row = jnp.maximum(inter, jnp.max(dm, axis=1, keepdims=True))
        wts = jnp.exp(dm - m_row)
        sc = lax.dot_general(qh, kh, (((1,), (1,)), ((), ())),
                             preferred_element_type=F32) * wts
        inter_w = jnp.exp(inter - m_row)
        cq = lax.dot_general(qh, c_state.astype(BF16), (((1,), (1,)), ((), ())),
                             preferred_element_type=F32)
        num = jnp.dot(sc.astype(BF16), vh.astype(BF16),
                      preferred_element_type=F32) + inter_w * cq
        nq = jnp.sum(qh.astype(F32) * n_state, axis=1, keepdims=True)
        den = jnp.sum(sc, axis=1, keepdims=True) + inter_w * nq
        h_c = num / jnp.maximum(jnp.abs(den), jnp.exp(-m_row))

        a_col = g - bcum_col + i_col
        a_row = g - bcum_row + i_row
        m_new = jnp.maximum(g + m, jnp.max(a_row, axis=1, keepdims=True))
        wk_col = jnp.exp(a_col - m_new)
        wk_row = jnp.exp(a_row - m_new)
        decay = jnp.exp(g + m - m_new)
        c_sc[h] = decay * c_state + lax.dot_general(
            (vh * wk_col).astype(BF16), kh, (((0,), (0,)), ((), ())),
            preferred_element_type=F32)
        n_sc[h] = decay * n_state + jnp.dot(wk_row.astype(BF16), kh,
                                            preferred_element_type=F32)
        m_sc[h] = m_new

        vs = slice(h * dv, (h + 1) * dv)
        y = jax.nn.sigmoid(o_ref[:, vs]) * h_c * jax.nn.silu(z_ref[:, vs])
        y_ref[:, vs] = y.astype(y_ref.dtype)


def mlstm_mixer(qk, vo, z, gates_col, gates_row, bias_col, bias_row, n_heads):
    s = qk.shape[0]
    length = min(MLSTM_CHUNK, s)
    qk_w = n_heads * MLSTM_QK_DIM
    v_w = n_heads * MLSTM_V_DIM
    return pl.pallas_call(
        _mlstm_kernel,
        out_shape=jax.ShapeDtypeStruct((s, v_w), BF16),
        grid=(s // length,),
        in_specs=[pl.BlockSpec((length, qk_w), lambda c: (c, 0)),
                  pl.BlockSpec((length, qk_w), lambda c: (c, 1)),
                  pl.BlockSpec((length, v_w), lambda c: (c, 0)),
                  pl.BlockSpec((length, v_w), lambda c: (c, 1)),
                  pl.BlockSpec((length, v_w), lambda c: (c, 0)),
                  pl.BlockSpec((length, LANES), lambda c: (c, 0)),
                  pl.BlockSpec((2 * n_heads, length), lambda c: (0, c)),
                  pl.BlockSpec((1, LANES), lambda c: (0, 0)),
                  pl.BlockSpec((2 * n_heads, 1), lambda c: (0, 0))],
        out_specs=pl.BlockSpec((length, v_w), lambda c: (c, 0)),
        scratch_shapes=[pltpu.VMEM((n_heads, MLSTM_V_DIM, MLSTM_QK_DIM), F32),
                        pltpu.VMEM((n_heads, 1, MLSTM_QK_DIM), F32),
                        pltpu.VMEM((n_heads, 1, 1), F32)],
        compiler_params=_params("arbitrary"),
        name="mlstm_mixer",
    )(qk, qk, vo, vo, z, gates_col, gates_row, bias_row, bias_col)


def mlstm_branch(u, w_in, b_i, b_f, w_out, h_res):
    n_heads = b_i.shape[0]
    qk_w = n_heads * MLSTM_QK_DIM
    v_w = n_heads * MLSTM_V_DIM
    assert 2 * n_heads <= LANES and (2 * n_heads) % SUBLANES == 0
    o_qk = 2 * qk_w
    o_vo = o_qk + 2 * v_w
    o_g = o_vo + 2 * n_heads
    w16 = w_in.astype(BF16)
    w_g = jnp.pad(w_in[:, o_vo:o_g], ((0, 0), (0, LANES - 2 * n_heads))).astype(BF16)
    bias = jnp.concatenate([b_i, b_f])
    bias_row = jnp.pad(bias, (0, LANES - 2 * n_heads)).reshape(1, LANES)
    bias_col = bias.reshape(2 * n_heads, 1)
    qk = matmul(u, w16, BF16, cols=(0, o_qk))
    vo = matmul(u, w16, F32, cols=(o_qk, 2 * v_w))
    z = matmul(u, w16[:, o_g:], F32)
    gates_col = matmul(u, w_g, F32)
    gates_row = gates_col[:, :2 * n_heads].T
    y = mlstm_mixer(qk, vo, z, gates_col, gates_row, bias_col, bias_row, n_heads)
    return matmul(y, w_out.astype(BF16), F32, residual=h_res)


def kernel(x, norm_g, fox_w_in, fox_b_f, fox_w_out, lru_w_in, lru_conv_w, lru_conv_b, lru_w_a,
           lru_b_a, lru_w_x, lru_b_x, lru_lambda, lru_w_out, mlstm_w_in, mlstm_b_i, mlstm_b_f,
           mlstm_w_out, final_norm_g):
    bsz, s, d = x.shape
    depth = norm_g.shape[0]
    rows = x.reshape(bsz * s, d)
    outs = []
    for b in range(bsz):
        h = rows[b * s:(b + 1) * s]
        for i in range(depth):
            j = i // N_MIXERS
            u = rms_norm(h, norm_g[i], BF16)
            kind = i % N_MIXERS
            if kind == 0:
                h = fox_branch(u, fox_w_in[j], fox_b_f[j], fox_w_out[j], h)
            elif kind == 1:
                h = lru_branch(u, lru_w_in[j], lru_conv_w[j], lru_conv_b[j], lru_w_a[j],
                               lru_b_a[j].reshape(-1), lru_w_x[j], lru_b_x[j].reshape(-1),
                               lru_lambda[j], lru_w_out[j], h)
            else:
                h = mlstm_branch(u, mlstm_w_in[j], mlstm_b_i[j], mlstm_b_f[j], mlstm_w_out[j], h)
        outs.append(rms_norm(h, final_norm_g, x.dtype))
    return jnp.concatenate(outs, axis=0).reshape(bsz, s, d)
```

```python
import functools
import math

import jax
import jax.numpy as jnp
from jax import lax
from jax.experimental import pallas as pl
from jax.experimental.pallas import tpu as pltpu

F32 = jnp.float32
BF16 = jnp.bfloat16

RMS_EPS = 1e-6
NEG_INF = -1e30
LOG2E = math.log2(math.e)
EXP2_ZERO_BELOW = -160.0
FIXED_STABILISER_MAX_BOUND = 40.0
AUG_ROWS = 16
N_MIXERS = 3

FOX_HEAD_DIM = 128
LRU_BLOCK_SIZE = 256
CONV_WIDTH = 4
LRU_C = 8.0
MLSTM_V_DIM = 512
MLSTM_QK_DIM = 256

LANES = 128
SUBLANES = 8
V7X_VMEM_LIMIT_BYTES = 56 * 1024 * 1024

NORM_ROWS = 256
MM_TM = 1024
MM_TN = 1024
GATE_ROWS = 512
ATTN_TQ = 1024
ATTN_TK = 1024
LRU_ROWS = 256
MLSTM_CHUNK = 256


def _params(*sem):
    return pltpu.CompilerParams(dimension_semantics=sem,
                                vmem_limit_bytes=V7X_VMEM_LIMIT_BYTES)


def _norm_kernel(x_ref, g_ref, o_ref):
    x = x_ref[...]
    y = x * lax.rsqrt(jnp.mean(x * x, axis=-1, keepdims=True) + RMS_EPS)
    o_ref[...] = (y * g_ref[...]).astype(o_ref.dtype)


def rms_norm(x, g, out_dtype):
    s, d = x.shape
    tr = min(NORM_ROWS, s)
    return pl.pallas_call(
        _norm_kernel,
        out_shape=jax.ShapeDtypeStruct((s, d), out_dtype),
        grid=(s // tr,),
        in_specs=[pl.BlockSpec((tr, d), lambda i: (i, 0)),
                  pl.BlockSpec((1, d), lambda i: (0, 0))],
        out_specs=pl.BlockSpec((tr, d), lambda i: (i, 0)),
        compiler_params=_params("parallel"),
        name="rms_norm",
    )(x, g.reshape(1, d))


def _mm_kernel(a_ref, w_ref, o_ref):
    o_ref[...] = jnp.dot(a_ref[...], w_ref[...],
                         preferred_element_type=F32).astype(o_ref.dtype)


def _mm_res_kernel(a_ref, w_ref, r_ref, o_ref):
    o_ref[...] = r_ref[...] + jnp.dot(a_ref[...], w_ref[...],
                                      preferred_element_type=F32)


def _col_window(w, cols, tn_max):
    start, n = (0, w.shape[1]) if cols is None else cols
    tn = min(tn_max, n)
    assert start % tn == 0 and n % tn == 0
    return start // tn, n, tn


def matmul(a, w, out_dtype, residual=None, cols=None):
    m, k = a.shape
    j0, n, tn = _col_window(w, cols, MM_TN)
    tm = min(MM_TM, m)
    in_specs = [pl.BlockSpec((tm, k), lambda i, j: (i, 0)),
                pl.BlockSpec((k, tn), lambda i, j: (0, j + j0))]
    args = [a, w]
    body = _mm_kernel
    if residual is not None:
        in_specs.append(pl.BlockSpec((tm, tn), lambda i, j: (i, j)))
        args.append(residual)
        body = _mm_res_kernel
    return pl.pallas_call(
        body,
        out_shape=jax.ShapeDtypeStruct((m, n), out_dtype),
        grid=(m // tm, n // tn),
        in_specs=in_specs,
        out_specs=pl.BlockSpec((tm, tn), lambda i, j: (i, j)),
        compiler_params=_params("parallel", "parallel"),
        name="matmul_res" if residual is not None else "matmul",
    )(*args)


def _mm_t_kernel(a_ref, w_ref, o_ref):
    o_ref[...] = lax.dot_general(w_ref[...], a_ref[...], (((0,), (1,)), ((), ())),
                                 preferred_element_type=F32).astype(o_ref.dtype)


def matmul_t(a, w, out_dtype, cols=None):
    m, k = a.shape
    j0, n, tn = _col_window(w, cols, MM_TN)
    tm = min(MM_TM, m)
    return pl.pallas_call(
        _mm_t_kernel,
        out_shape=jax.ShapeDtypeStruct((n, m), out_dtype),
        grid=(m // tm, n // tn),
        in_specs=[pl.BlockSpec((tm, k), lambda i, j: (i, 0)),
                  pl.BlockSpec((k, tn), lambda i, j: (0, j + j0))],
        out_specs=pl.BlockSpec((tn, tm), lambda i, j: (j, i)),
        compiler_params=_params("parallel", "parallel"),
        name="matmul_t",
    )(a, w)


def _lower_tri(n):
    r = lax.broadcasted_iota(jnp.int32, (n, n), 0)
    c = lax.broadcasted_iota(jnp.int32, (n, n), 1)
    return (c <= r).astype(F32)


def _upper_tri(n):
    r = lax.broadcasted_iota(jnp.int32, (n, n), 0)
    c = lax.broadcasted_iota(jnp.int32, (n, n), 1)
    return (r <= c).astype(F32)


def _dot_f32(a, b):
    return jnp.dot(a, b, precision=lax.Precision.HIGHEST, preferred_element_type=F32)


def _fox_gate_kernel(f_ref, b_ref, fcol_ref, frow_ref, carry_ref):
    @pl.when(pl.program_id(0) == 0)
    def _():
        carry_ref[...] = jnp.zeros_like(carry_ref)

    tb = f_ref.shape[0]
    logf = jax.nn.log_sigmoid(f_ref[...] + b_ref[...])
    cs = _dot_f32(_lower_tri(tb), logf) + carry_ref[...]
    fcol_ref[...] = cs
    frow_ref[...] = cs.T
    carry_ref[...] = cs[tb - 1:tb, :]


def fox_gates(f_pre, b_pad):
    s = f_pre.shape[0]
    tb = min(GATE_ROWS, s)
    return pl.pallas_call(
        _fox_gate_kernel,
        out_shape=(jax.ShapeDtypeStruct((s, LANES), F32),
                   jax.ShapeDtypeStruct((LANES, s), F32)),
        grid=(s // tb,),
        in_specs=[pl.BlockSpec((tb, LANES), lambda i: (i, 0)),
                  pl.BlockSpec((1, LANES), lambda i: (0, 0))],
        out_specs=(pl.BlockSpec((tb, LANES), lambda i: (i, 0)),
                   pl.BlockSpec((LANES, tb), lambda i: (0, i))),
        scratch_shapes=[pltpu.VMEM((1, LANES), F32)],
        compiler_params=_params("arbitrary"),
        name="fox_gates",
    )(f_pre, b_pad)


def _split3(x):
    hi = x.astype(BF16)
    rest = x - hi.astype(F32)
    mid = rest.astype(BF16)
    lo = (rest - mid.astype(F32)).astype(BF16)
    return hi.astype(F32), mid.astype(F32), lo.astype(F32)


def _fox_attn_kernel(q_ref, kt_ref, v_ref, z_ref, fcol_ref, frow_ref, o_ref,
                     m_sc, l_sc, acc_sc, kmax_sc, qaug_sc, kaug_sc, *, scale, tk):
    h = pl.program_id(0)
    i = pl.program_id(1)
    tq, dh = q_ref.shape
    s_len = kt_ref.shape[1]
    c = scale * LOG2E
    q = q_ref[...]
    lane = lax.broadcasted_iota(jnp.int32, (tq, LANES), 1)
    fq = jnp.sum(jnp.where(lane == h, fcol_ref[...], 0.0), axis=1, keepdims=True) * LOG2E
    n_full = (i * tq) // tk

    @pl.when(i == 0)
    def _():
        kf = kt_ref[...].astype(F32)
        k_sq = jnp.sum(kf * kf, axis=0, keepdims=True)
        kmax_sc[...] = jnp.sqrt(jnp.max(k_sq, axis=1, keepdims=True))
        hi, mid, lo = _split3(frow_ref[...] * (-1.0 / scale))
        sub = lax.broadcasted_iota(jnp.int32, (AUG_ROWS, s_len), 0)
        extra = jnp.where(sub < 3, 1.0,
                          jnp.where(sub == 3, hi, jnp.where(sub == 4, mid,
                                                            jnp.where(sub == 5, lo, 0.0))))
        kaug_sc[:dh, :] = kt_ref[...]
        kaug_sc[dh:dh + AUG_ROWS, :] = extra.astype(BF16)
        kaug_sc[dh + AUG_ROWS:, :] = jnp.zeros((dh - AUG_ROWS, s_len), BF16)

    qf = q.astype(F32)
    row_bound = c * jnp.sqrt(jnp.sum(qf * qf, axis=1, keepdims=True)) * kmax_sc[...]
    bound = jnp.max(row_bound, axis=0, keepdims=True)
    use_fixed = bound[0, 0] <= FIXED_STABILISER_MAX_BOUND

    threshold = jnp.where(use_fixed, EXP2_ZERO_BELOW, EXP2_ZERO_BELOW - 2.0 * bound)
    fq_max = jnp.max(fq, axis=0, keepdims=True)
    pos = lax.broadcasted_iota(jnp.int32, (1, s_len), 1)
    live = (fq_max - frow_ref[...] * LOG2E) >= threshold
    first_live = jnp.min(jnp.where(live, pos, s_len), axis=1, keepdims=True)
    j_start = jnp.minimum(first_live[0, 0] // tk, n_full)

    l_sc[...] = jnp.zeros_like(l_sc)
    acc_sc[...] = jnp.zeros_like(acc_sc)
    qaug_sc[:, :dh] = q

    def run(fixed):
        hi, mid, lo = _split3((fq - row_bound if fixed else fq) * (1.0 / c))
        extra = jnp.where(lane == 0, hi, jnp.where(lane == 1, mid, jnp.where(
            lane == 2, lo, jnp.where(lane < 6, 1.0, 0.0))))
        qaug_sc[:, dh:] = extra.astype(BF16)
        if not fixed:
            m_sc[...] = jnp.full_like(m_sc, NEG_INF)

        def block(r0, nr, off, nk, masked):
            rows = slice(r0, r0 + nr)
            v = v_ref[pl.ds(off, nk), :]
            s = jnp.dot(qaug_sc[rows, :], kaug_sc[:, pl.ds(off, nk)],
                        preferred_element_type=F32)
            t = s * c
            if masked:
                row = lax.broadcasted_iota(jnp.int32, (nr, nk), 0) + (i * tq + r0)
                col = lax.broadcasted_iota(jnp.int32, (nr, nk), 1) + off
                t = jnp.where(col <= row, t, NEG_INF)
            if fixed:
                p = jnp.exp2(t)
                l_sc[rows, :] += jnp.sum(p, axis=1, keepdims=True)
                acc_sc[rows, :] += jnp.dot(p.astype(BF16), v, preferred_element_type=F32)
            else:
                m_prev = m_sc[rows, :]
                m_new = jnp.maximum(m_prev, jnp.max(t, axis=1, keepdims=True))
                alpha = jnp.exp2(m_prev - m_new)
                p = jnp.exp2(t - jnp.tile(m_new, (1, nk // LANES)))
                l_sc[rows, :] = alpha * l_sc[rows, :] + jnp.sum(p, axis=1, keepdims=True)
                acc_sc[rows, :] = alpha * acc_sc[rows, :] + jnp.dot(
                    p.astype(BF16), v, preferred_element_type=F32)
                m_sc[rows, :] = m_new

        def below_diagonal(j, carry):
            block(0, tq, pl.multiple_of(j * tk, tk), tk, False)
            return carry

        lax.fori_loop(j_start, n_full, below_diagonal, 0)
        diag = pl.multiple_of(n_full * tk, tk)
        if tk == tq and (tq // 2) % LANES == 0:
            half = tq // 2
            block(0, tq, diag, half, True)
            block(half, half, pl.multiple_of(diag + half, half), half, True)
        else:
            block(0, tq, diag, tk, True)

    pl.when(use_fixed)(functools.partial(run, True))
    pl.when(jnp.logical_not(use_fixed))(functools.partial(run, False))

    o = acc_sc[...] / l_sc[...]
    o_ref[...] = (o * jax.nn.silu(z_ref[...])).astype(o_ref.dtype)


def fox_attention(q, k_t, v, z, fcol, frow3, n_heads):
    s = q.shape[0]
    dh = FOX_HEAD_DIM
    tq = min(ATTN_TQ, s)
    tk = min(ATTN_TK, s)
    assert tk % tq == 0 and tk % LANES == 0 and dh == LANES
    kernel = functools.partial(_fox_attn_kernel, scale=dh ** -0.5, tk=tk)
    return pl.pallas_call(
        kernel,
        out_shape=jax.ShapeDtypeStruct((s, n_heads * dh), BF16),
        grid=(n_heads, s // tq),
        in_specs=[pl.BlockSpec((tq, dh), lambda h, i: (i, h)),
                  pl.BlockSpec((dh, s), lambda h, i: (h, 0)),
                  pl.BlockSpec((s, dh), lambda h, i: (0, h)),
                  pl.BlockSpec((tq, dh), lambda h, i: (i, h)),
                  pl.BlockSpec((tq, LANES), lambda h, i: (i, 0)),
                  pl.BlockSpec((None, 1, s), lambda h, i: (h, 0, 0))],
        out_specs=pl.BlockSpec((tq, dh), lambda h, i: (i, h)),
        scratch_shapes=[pltpu.VMEM((tq, LANES), F32), pltpu.VMEM((tq, LANES), F32),
                        pltpu.VMEM((tq, dh), F32), pltpu.VMEM((1, 1), F32),
                        pltpu.VMEM((tq, 2 * dh), BF16), pltpu.VMEM((2 * dh, s), BF16)],
        compiler_params=_params("parallel", "arbitrary"),
        name="fox_attention",
    )(q, k_t, v, z, fcol, frow3)


def fox_branch(u, w_in, b_f, w_out, h_res):
    s, d = u.shape
    n_heads = b_f.shape[0]
    width = n_heads * FOX_HEAD_DIM
    assert n_heads <= LANES
    w16 = w_in.astype(BF16)
    w_f = jnp.pad(w_in[:, 4 * width:], ((0, 0), (0, LANES - n_heads))).astype(BF16)
    b_pad = jnp.pad(b_f, (0, LANES - n_heads)).reshape(1, LANES)
    q = matmul(u, w16, BF16, cols=(0, width))
    k_t = matmul_t(u, w16, BF16, cols=(width, width))
    v = matmul(u, w16, BF16, cols=(2 * width, width))
    z = matmul(u, w16, F32, cols=(3 * width, width))
    f_pre = matmul(u, w_f, F32)
    fcol, frow = fox_gates(f_pre, b_pad)
    frow3 = frow[:n_heads].reshape(n_heads, 1, s)
    gated = fox_attention(q, k_t, v, z, fcol, frow3, n_heads)
    return matmul(gated, w_out.astype(BF16), F32, residual=h_res)


def _lru_kernel(x_ref, halo_ref, z_ref, cw_ref, cb_ref, wa_ref, ba_ref, wx_ref, bx_ref,
                lam_ref, o_ref, a_sc, b_sc, h_sc, state_sc):
    i = pl.program_id(0)
    t, width = x_ref.shape
    n_blocks = wa_ref.shape[0]
    bs = LRU_BLOCK_SIZE

    @pl.when(i == 0)
    def _():
        state_sc[...] = jnp.zeros_like(state_sc)

    halo = jnp.where(i == 0, 0.0, halo_ref[...])
    ext = jnp.concatenate([halo, x_ref[...]], axis=0)
    xc = cb_ref[...] + x_ref[...] * cw_ref[CONV_WIDTH - 1:CONV_WIDTH, :]
    for j in range(CONV_WIDTH - 1):
        shifted = pltpu.roll(ext, CONV_WIDTH - 1 - j, axis=0)[SUBLANES:, :]
        xc = xc + shifted * cw_ref[j:j + 1, :]

    for n in range(n_blocks):
        sl = slice(n * bs, (n + 1) * bs)
        xn = xc[:, sl]
        xn16 = xn.astype(BF16)
        r = jax.nn.sigmoid(jnp.dot(xn16, wa_ref[n], preferred_element_type=F32) + ba_ref[:, sl])
        ig = jax.nn.sigmoid(jnp.dot(xn16, wx_ref[n], preferred_element_type=F32) + bx_ref[:, sl])
        log_a = -LRU_C * r * jax.nn.softplus(-lam_ref[:, sl])
        a = jnp.exp(log_a)
        a_sc[:, sl] = a
        b_sc[:, sl] = jnp.sqrt(-jnp.tanh(log_a) * (a * a + 1.0)) * (ig * xn)

    def step(r, hprev):
        hnew = a_sc[pl.ds(r, 1), :] * hprev + b_sc[pl.ds(r, 1), :]
        h_sc[pl.ds(r, 1), :] = hnew
        return hnew

    state_sc[...] = lax.fori_loop(0, t, step, state_sc[...], unroll=8)
    o_ref[...] = (h_sc[...] * jax.nn.silu(z_ref[...])).astype(o_ref.dtype)


def lru_mixer(p, conv_w, conv_b, w_a, b_a, w_x, b_x, lam):
    s = p.shape[0]
    width = p.shape[1] // 2
    n_blocks = w_a.shape[0]
    t = min(LRU_ROWS, s)
    halo_blocks = t // SUBLANES
    row = lambda v: v.reshape(1, width)
    const2 = lambda i: (0, 0)
    const3 = lambda i: (0, 0, 0)
    return pl.pallas_call(
        _lru_kernel,
        out_shape=jax.ShapeDtypeStruct((s, width), BF16),
        grid=(s // t,),
        in_specs=[pl.BlockSpec((t, width), lambda i: (i, 0)),
                  pl.BlockSpec((SUBLANES, width),
                               lambda i: (jnp.maximum(i * halo_blocks - 1, 0), 0)),
                  pl.BlockSpec((t, width), lambda i: (i, 1)),
                  pl.BlockSpec((CONV_WIDTH, width), const2),
                  pl.BlockSpec((1, width), const2),
                  pl.BlockSpec((n_blocks, LRU_BLOCK_SIZE, LRU_BLOCK_SIZE), const3),
                  pl.BlockSpec((1, width), const2),
                  pl.BlockSpec((n_blocks, LRU_BLOCK_SIZE, LRU_BLOCK_SIZE), const3),
                  pl.BlockSpec((1, width), const2),
                  pl.BlockSpec((1, width), const2)],
        out_specs=pl.BlockSpec((t, width), lambda i: (i, 0)),
        scratch_shapes=[pltpu.VMEM((t, width), F32), pltpu.VMEM((t, width), F32),
                        pltpu.VMEM((t, width), F32), pltpu.VMEM((1, width), F32)],
        compiler_params=_params("arbitrary"),
        name="lru_mixer",
    )(p, p, p, conv_w, row(conv_b), w_a.astype(BF16), row(b_a), w_x.astype(BF16), row(b_x),
      row(lam))


def lru_branch(u, w_in, conv_w, conv_b, w_a, b_a, w_x, b_x, lam, w_out, h_res):
    p = matmul(u, w_in.astype(BF16), F32)
    gated = lru_mixer(p, conv_w, conv_b, w_a, b_a, w_x, b_x, lam)
    return matmul(gated, w_out.astype(BF16), F32, residual=h_res)


def _mlstm_kernel(q_ref, k_ref, v_ref, o_ref, z_ref, gc_ref, gr_ref, bc_ref, br_ref, y_ref,
                  c_sc, n_sc, m_sc):
    n_heads = c_sc.shape[0]
    length = q_ref.shape[0]
    dk = MLSTM_QK_DIM
    dv = MLSTM_V_DIM

    @pl.when(pl.program_id(0) == 0)
    def _():
        c_sc[...] = jnp.zeros_like(c_sc)
        n_sc[...] = jnp.zeros_like(n_sc)
        m_sc[...] = jnp.zeros_like(m_sc)

    gcol = gc_ref[...] + bc_ref[...]
    grow = gr_ref[...] + br_ref[...]
    lane = lax.broadcasted_iota(jnp.int32, gcol.shape, 1)
    is_f_col = (lane >= n_heads) & (lane < 2 * n_heads)
    logf_col = jnp.where(is_f_col, jax.nn.log_sigmoid(gcol), 0.0)
    bcum_col_all = _dot_f32(_lower_tri(length), logf_col)
    logf_row = jax.nn.log_sigmoid(grow[n_heads:, :])
    bcum_row_all = _dot_f32(logf_row, _upper_tri(length))

    row = lax.broadcasted_iota(jnp.int32, (length, length), 0)
    col = lax.broadcasted_iota(jnp.int32, (length, length), 1)
    causal = col <= row

    for h in range(n_heads):
        qh = q_ref[:, h * dk:(h + 1) * dk] * (dk ** -0.5)
        kh = k_ref[:, h * dk:(h + 1) * dk]
        vh = v_ref[:, h * dv:(h + 1) * dv]
        i_col = gcol[:, h:h + 1]
        i_row = grow[h:h + 1, :]
        bcum_col = bcum_col_all[:, n_heads + h:n_heads + h + 1]
        bcum_row = bcum_row_all[h:h + 1, :]
        g = bcum_col[length - 1:length, :]
        m = m_sc[h]
        c_state = c_sc[h]
        n_state = n_sc[h]

        dm = jnp.where(causal, bcum_col - bcum_row + i_row, NEG_INF)
        inter = bcum_col + m
        m_row = jnp.maximum(inter, jnp.max(dm, axis=1, keepdims=True))
        wts = jnp.exp(dm - m_row)
        sc = lax.dot_general(qh, kh, (((1,), (1,)), ((), ())),
                             preferred_element_type=F32) * wts
        inter_w = jnp.exp(inter - m_row)
        cq = lax.dot_general(qh, c_state.astype(BF16), (((1,), (1,)), ((), ())),
                             preferred_element_type=F32)
        num = jnp.dot(sc.astype(BF16), vh.astype(BF16),
                      preferred_element_type=F32) + inter_w * cq
        nq = jnp.sum(qh.astype(F32) * n_state, axis=1, keepdims=True)
        den = jnp.sum(sc, axis=1, keepdims=True) + inter_w * nq
        h_c = num / jnp.maximum(jnp.abs(den), jnp.exp(-m_row))

        a_col = g - bcum_col + i_col
        a_row = g - bcum_row + i_row
        m_new = jnp.maximum(g + m, jnp.max(a_row, axis=1, keepdims=True))
        wk_col = jnp.exp(a_col - m_new)
        wk_row = jnp.exp(a_row - m_new)
        decay = jnp.exp(g + m - m_new)
        c_sc[h] = decay * c_state + lax.dot_general(
            (vh * wk_col).astype(BF16), kh, (((0,), (0,)), ((), ())),
            preferred_element_type=F32)
        n_sc[h] = decay * n_state + jnp.dot(wk_row.astype(BF16), kh,
                                            preferred_element_type=F32)
        m_sc[h] = m_new

        vs = slice(h * dv, (h + 1) * dv)
        y = jax.nn.sigmoid(o_ref[:, vs]) * h_c * jax.nn.silu(z_ref[:, vs])
        y_ref[:, vs] = y.astype(y_ref.dtype)


def mlstm_mixer(qk, vo, z, gates_col, gates_row, bias_col, bias_row, n_heads):
    s = qk.shape[0]
    length = min(MLSTM_CHUNK, s)
    qk_w = n_heads * MLSTM_QK_DIM
    v_w = n_heads * MLSTM_V_DIM
    return pl.pallas_call(
        _mlstm_kernel,
        out_shape=jax.ShapeDtypeStruct((s, v_w), BF16),
        grid=(s // length,),
        in_specs=[pl.BlockSpec((length, qk_w), lambda c: (c, 0)),
                  pl.BlockSpec((length, qk_w), lambda c: (c, 1)),
                  pl.BlockSpec((length, v_w), lambda c: (c, 0)),
                  pl.BlockSpec((length, v_w), lambda c: (c, 1)),
                  pl.BlockSpec((length, v_w), lambda c: (c, 0)),
                  pl.BlockSpec((length, LANES), lambda c: (c, 0)),
                  pl.BlockSpec((2 * n_heads, length), lambda c: (0, c)),
                  pl.BlockSpec((1, LANES), lambda c: (0, 0)),
                  pl.BlockSpec((2 * n_heads, 1), lambda c: (0, 0))],
        out_specs=pl.BlockSpec((length, v_w), lambda c: (c, 0)),
        scratch_shapes=[pltpu.VMEM((n_heads, MLSTM_V_DIM, MLSTM_QK_DIM), F32),
                        pltpu.VMEM((n_heads, 1, MLSTM_QK_DIM), F32),
                        pltpu.VMEM((n_heads, 1, 1), F32)],
        compiler_params=_params("arbitrary"),
        name="mlstm_mixer",
    )(qk, qk, vo, vo, z, gates_col, gates_row, bias_row, bias_col)


def mlstm_branch(u, w_in, b_i, b_f, w_out, h_res):
    n_heads = b_i.shape[0]
    qk_w = n_heads * MLSTM_QK_DIM
    v_w = n_heads * MLSTM_V_DIM
    assert 2 * n_heads <= LANES and (2 * n_heads) % SUBLANES == 0
    o_qk = 2 * qk_w
    o_vo = o_qk + 2 * v_w
    o_g = o_vo + 2 * n_heads
    w16 = w_in.astype(BF16)
    w_g = jnp.pad(w_in[:, o_vo:o_g], ((0, 0), (0, LANES - 2 * n_heads))).astype(BF16)
    bias = jnp.concatenate([b_i, b_f])
    bias_row = jnp.pad(bias, (0, LANES - 2 * n_heads)).reshape(1, LANES)
    bias_col = bias.reshape(2 * n_heads, 1)
    qk = matmul(u, w16, BF16, cols=(0, o_qk))
    vo = matmul(u, w16, F32, cols=(o_qk, 2 * v_w))
    z = matmul(u, w16[:, o_g:], F32)
    gates_col = matmul(u, w_g, F32)
    gates_row = gates_col[:, :2 * n_heads].T
    y = mlstm_mixer(qk, vo, z, gates_col, gates_row, bias_col, bias_row, n_heads)
    return matmul(y, w_out.astype(BF16), F32, residual=h_res)


def kernel(x, norm_g, fox_w_in, fox_b_f, fox_w_out, lru_w_in, lru_conv_w, lru_conv_b, lru_w_a,
           lru_b_a, lru_w_x, lru_b_x, lru_lambda, lru_w_out, mlstm_w_in, mlstm_b_i, mlstm_b_f,
           mlstm_w_out, final_norm_g):
    bsz, s, d = x.shape
    depth = norm_g.shape[0]
    rows = x.reshape(bsz * s, d)
    outs = []
    for b in range(bsz):
        h = rows[b * s:(b + 1) * s]
        for i in range(depth):
            j = i // N_MIXERS
            u = rms_norm(h, norm_g[i], BF16)
            kind = i % N_MIXERS
            if kind == 0:
                h = fox_branch(u, fox_w_in[j], fox_b_f[j], fox_w_out[j], h)
            elif kind == 1:
                h = lru_branch(u, lru_w_in[j], lru_conv_w[j], lru_conv_b[j], lru_w_a[j],
                               lru_b_a[j].reshape(-1), lru_w_x[j], lru_b_x[j].reshape(-1),
                               lru_lambda[j], lru_w_out[j], h)
            else:
                h = mlstm_branch(u, mlstm_w_in[j], mlstm_b_i[j], mlstm_b_f[j], mlstm_w_out[j], h)
        outs.append(rms_norm(h, final_norm_g, x.dtype))
    return jnp.concatenate(outs, axis=0).reshape(bsz, s, d)
```
